```python
import jax, jax.numpy as jnp
from jax import lax
import numpy as np

D_MODEL = 1024
BATCH = 32
SEQ = 2048
DEPTH = 1

N_META = 16
CONV_DIM = 512
CONV_WIDTH = 31
RET_HEADS = 4
RET_DK = 128
RET_DV = 256
RET_CHUNK = 128
ROPE_BASE = 10000.0
PEER_HEADS = 8
PEER_NKEYS = 128
PEER_NEXPERTS = PEER_NKEYS * PEER_NKEYS
PEER_DKEY = 256
PEER_TOPK = 16
PEER_BLOCK = 256
EPS = 1e-6

QK_W = RET_HEADS * RET_DK
V_W = RET_HEADS * RET_DV
IN_COLS = 2 * CONV_DIM + 2 * QK_W + 2 * V_W + 2 * D_MODEL
SPLIT_POINTS = (
    CONV_DIM,
    2 * CONV_DIM,
    2 * CONV_DIM + QK_W,
    2 * CONV_DIM + 2 * QK_W,
    2 * CONV_DIM + 2 * QK_W + V_W,
    2 * CONV_DIM + 2 * QK_W + 2 * V_W,
    2 * CONV_DIM + 2 * QK_W + 2 * V_W + D_MODEL,
)

kernel_name = "hybrid_conv_retention_peer_block"


def rmsnorm(x, g):
    xf = x.astype(jnp.float32)
    y = xf * lax.rsqrt(jnp.mean(xf * xf, axis=-1, keepdims=True) + EPS)
    return (y * g.astype(jnp.float32)).astype(x.dtype)


def layernorm(x, g, b):
    xf = x.astype(jnp.float32)
    mu = jnp.mean(xf, axis=-1, keepdims=True)
    var = jnp.mean(jnp.square(xf - mu), axis=-1, keepdims=True)
    y = (xf - mu) * lax.rsqrt(var + EPS)
    return (y * g.astype(jnp.float32) + b.astype(jnp.float32)).astype(x.dtype)


def rotary(t, pos):
    half = t.shape[-1] // 2
    freqs = ROPE_BASE ** (-jnp.arange(half, dtype=jnp.float32) / half)
    ang = pos[:, None] * freqs[None, :]
    cos = jnp.cos(ang)[None, :, None, :]
    sin = jnp.sin(ang)[None, :, None, :]
    t1, t2 = t[..., :half], t[..., half:]
    return jnp.concatenate([t1 * cos - t2 * sin, t1 * sin + t2 * cos], axis=-1)


def conv_module(a, b, dw_w, dw_b, ln_g, ln_b, proj_w):
    h = a * jax.nn.sigmoid(b)
    h = lax.conv_general_dilated(
        h, dw_w[:, None, :].astype(h.dtype), window_strides=(1,),
        padding=[(CONV_WIDTH - 1, 0)], dimension_numbers=("NWC", "WIO", "NWC"),
        feature_group_count=CONV_DIM) + dw_b
    h = jax.nn.silu(layernorm(h, ln_g, ln_b))
    return h @ proj_w


def retention(q, k, v, g, proj_w):
    B, L = q.shape[0], q.shape[1]
    out_dtype = v.dtype
    pos = jnp.arange(L, dtype=jnp.float32)
    q = rotary(q.astype(jnp.float32), pos)
    k = rotary(k.astype(jnp.float32), pos) * (RET_DK ** -0.5)
    v = v.astype(jnp.float32)
    pad = (-L) % RET_CHUNK
    Lp = L + pad
    n_chunks = Lp // RET_CHUNK

    def to_chunks(t):
        t = jnp.pad(t, ((0, 0), (pad, 0), (0, 0), (0, 0)))
        return t.reshape(B, n_chunks, RET_CHUNK, RET_HEADS, t.shape[-1]).transpose(1, 0, 3, 2, 4)

    qc, kc, vc = to_chunks(q), to_chunks(k), to_chunks(v)

    log_gamma = jnp.log(1.0 - 2.0 ** (-5.0 - jnp.arange(RET_HEADS, dtype=jnp.float32)))
    i = jnp.arange(RET_CHUNK, dtype=jnp.float32)
    rel = i[:, None] - i[None, :]
    decay_mask = jnp.where(rel >= 0, jnp.exp(log_gamma[:, None, None] * jnp.maximum(rel, 0.0)), 0.0)
    xi = jnp.exp(log_gamma[:, None] * (i[None, :] + 1.0))
    zeta = jnp.exp(log_gamma[:, None] * (RET_CHUNK - 1.0 - i[None, :]))
    gamma_c = jnp.exp(log_gamma * RET_CHUNK)

    def step(state, xs):
        qb, kb, vb = xs
        scores = jnp.einsum("bhid,bhjd->bhij", qb, kb) * decay_mask
        inner = jnp.einsum("bhij,bhjv->bhiv", scores, vb)
        cross = jnp.einsum("bhid,bhdv->bhiv", qb, state) * xi[None, :, :, None]
        state = state * gamma_c[None, :, None, None] + jnp.einsum(
            "bhjd,bhjv->bhdv", kb * zeta[None, :, :, None], vb)
        return state, inner + cross

    state0 = jnp.zeros((B, RET_HEADS, RET_DK, RET_DV), jnp.float32)
    _, o = lax.scan(step, state0, (qc, kc, vc))
    o = o.transpose(1, 0, 3, 2, 4).reshape(B, Lp, RET_HEADS, RET_DV)[:, pad:]
    o = o * lax.rsqrt(jnp.mean(o * o, axis=-1, keepdims=True) + EPS)
    o = o.reshape(B, L, V_W) * jax.nn.silu(g.astype(jnp.float32))
    return o.astype(out_dtype) @ proj_w


def peer(xn, w_q, sub_keys, u_tab, v_tab):
    B, L, D = xn.shape
    T = B * L
    n_blocks = -(-T // PEER_BLOCK)
    t = jnp.pad(xn.reshape(T, D), ((0, n_blocks * PEER_BLOCK - T), (0, 0)))
    blocks = t.reshape(n_blocks, PEER_BLOCK, D)

    def block_fn(xb):
        q = (xb @ w_q).reshape(PEER_BLOCK, PEER_HEADS, 2, PEER_DKEY // 2)
        s = jnp.einsum("thpd,hpkd->thpk", q, sub_keys)
        s_top, i_top = lax.top_k(s, PEER_TOPK)
        cand = s_top[:, :, 0, :, None] + s_top[:, :, 1, None, :]
        c_top, c_idx = lax.top_k(cand.reshape(PEER_BLOCK, PEER_HEADS, PEER_TOPK * PEER_TOPK), PEER_TOPK)
        ia = jnp.take_along_axis(i_top[:, :, 0], c_idx // PEER_TOPK, axis=-1)
        ib = jnp.take_along_axis(i_top[:, :, 1], c_idx % PEER_TOPK, axis=-1)
        expert = ia * PEER_NKEYS + ib
        gate = jax.nn.softmax(c_top.astype(jnp.float32), axis=-1)
        u = jnp.take(u_tab, expert, axis=0)
        act = jax.nn.gelu(jnp.einsum("thkd,td->thk", u, xb).astype(jnp.float32))
        w = (gate * act).astype(xb.dtype)
        return jnp.einsum("thk,thkd->td", w, jnp.take(v_tab, expert, axis=0))

    out = lax.map(block_fn, blocks).reshape(n_blocks * PEER_BLOCK, D)[:T]
    return out.reshape(B, L, D)


def setup_inputs(seed: int = 0) -> dict:
    key = jax.random.key(seed)
    ks = jax.random.split(key, 20)
    f32 = jnp.float32
    nrm = lambda k, shape, s: jax.random.normal(k, shape, f32) * s
    return {
        "x": nrm(ks[0], (BATCH, SEQ, D_MODEL), 1.0),
        "meta_tokens": nrm(ks[1], (N_META, D_MODEL), 1.0),
        "norm_mix_g": 1.0 + nrm(ks[2], (DEPTH, D_MODEL), 0.01),
        "w_in": nrm(ks[3], (DEPTH, D_MODEL, IN_COLS), D_MODEL ** -0.5),
        "conv_dw_w": nrm(ks[4], (DEPTH, CONV_WIDTH, CONV_DIM), CONV_WIDTH ** -0.5),
        "conv_dw_b": nrm(ks[5], (DEPTH, CONV_DIM), 0.01),
        "conv_ln_g": 1.0 + nrm(ks[6], (DEPTH, CONV_DIM), 0.01),
        "conv_ln_b": nrm(ks[7], (DEPTH, CONV_DIM), 0.01),
        "conv_proj_w": nrm(ks[8], (DEPTH, CONV_DIM, D_MODEL), CONV_DIM ** -0.5),
        "ret_proj_w": nrm(ks[9], (DEPTH, V_W, D_MODEL), V_W ** -0.5),
        "w_out": nrm(ks[10], (DEPTH, D_MODEL, D_MODEL), D_MODEL ** -0.5),
        "norm_ffn_g": 1.0 + nrm(ks[11], (DEPTH, D_MODEL), 0.01),
        "peer_w_q": nrm(ks[12], (DEPTH, D_MODEL, PEER_HEADS * PEER_DKEY), D_MODEL ** -0.5),
        "peer_sub_keys": nrm(ks[13], (DEPTH, PEER_HEADS, 2, PEER_NKEYS, PEER_DKEY // 2), (PEER_DKEY // 2) ** -0.5),
        "peer_u": nrm(ks[14], (DEPTH, PEER_NEXPERTS, D_MODEL), D_MODEL ** -0.5),
        "peer_v": nrm(ks[15], (DEPTH, PEER_NEXPERTS, D_MODEL), 0.5),
        "norm_final_g": 1.0 + nrm(ks[16], (D_MODEL,), 0.01),
    }


def reference(x, meta_tokens, norm_mix_g, w_in, conv_dw_w, conv_dw_b, conv_ln_g, conv_ln_b,
              conv_proj_w, ret_proj_w, w_out, norm_ffn_g, peer_w_q, peer_sub_keys,
              peer_u, peer_v, norm_final_g):
    B = x.shape[0]
    meta = jnp.broadcast_to(meta_tokens[None].astype(x.dtype), (B, N_META, D_MODEL))
    h = jnp.concatenate([meta, x], axis=1)
    L = h.shape[1]
    for l in range(DEPTH):
        xn = rmsnorm(h, norm_mix_g[l])
        z = xn @ w_in[l]
        a, b, q, k, v, g, gate_c, gate_r = jnp.split(z, SPLIT_POINTS, axis=-1)
        conv_out = conv_module(a, b, conv_dw_w[l], conv_dw_b[l], conv_ln_g[l], conv_ln_b[l], conv_proj_w[l])
        ret_out = retention(q.reshape(B, L, RET_HEADS, RET_DK), k.reshape(B, L, RET_HEADS, RET_DK),
                            v.reshape(B, L, RET_HEADS, RET_DV), g, ret_proj_w[l])
        merged = jax.nn.sigmoid(gate_c) * conv_out + jax.nn.sigmoid(gate_r) * ret_out
        h = h + merged @ w_out[l]
        h = h + peer(rmsnorm(h, norm_ffn_g[l]), peer_w_q[l], peer_sub_keys[l], peer_u[l], peer_v[l])
    y = rmsnorm(h, norm_final_g)
    return y[:, N_META:]
```

```python
import functools

import numpy as np
import jax
import jax.numpy as jnp
from jax.experimental import pallas as pl
from jax.experimental.pallas import tpu as pltpu

N_META = 16
CONV_DIM = 512
CONV_WIDTH = 31
RET_HEADS = 4
RET_DK = 128
RET_DV = 256
ROPE_BASE = 10000.0
PEER_HEADS = 8
PEER_NKEYS = 128
PEER_DKEY = 256
PEER_TOPK = 16
EPS = 1e-6

QK_W = RET_HEADS * RET_DK
V_W = RET_HEADS * RET_DV

V7X_VMEM_BYTES = 64 * 1024 * 1024
VMEM_LIMIT_BYTES = V7X_VMEM_BYTES - 8 * 1024 * 1024
LANES = 128
SUBLANES = 8

MIX_CHUNK = 256
HIST = 32
PEER_TOKENS = 512
PEER_EXPERTS = 1024

BF16 = jnp.bfloat16
F32 = jnp.float32


def _dot(a, b):
    return jnp.dot(a, b, preferred_element_type=F32)


def _dot_nt(a, b):
    return jax.lax.dot_general(a, b, (((1,), (1,)), ((), ())), preferred_element_type=F32)


def _sigmoid(x):
    return 1.0 / (1.0 + jnp.exp(-x))


def _mix_chunk(x, cos, sin, refs, state_ref, hbuf_ref):
    (g_mix, w_in, dw_w, dw_b, ln_g, ln_b, conv_proj, ret_proj, w_out, dmask, xi, zeta) = refs
    C = x.shape[0]

    ms = jnp.mean(x * x, axis=-1, keepdims=True)
    xn = (x * jax.lax.rsqrt(ms + EPS) * g_mix[...]).astype(BF16)

    a = _dot(xn, w_in[:, 0:CONV_DIM])
    b = _dot(xn, w_in[:, CONV_DIM:2 * CONV_DIM])
    hbuf_ref[HIST:HIST + C, :] = a * _sigmoid(b)
    y = jnp.zeros((C, CONV_DIM), F32) + dw_b[...]
    for w in range(CONV_WIDTH):
        off = HIST - (CONV_WIDTH - 1) + w
        y = y + hbuf_ref[off:off + C, :] * dw_w[w:w + 1, :]
    hbuf_ref[0:HIST, :] = hbuf_ref[C:C + HIST, :]
    mu = jnp.mean(y, axis=-1, keepdims=True)
    yc = y - mu
    var = jnp.mean(yc * yc, axis=-1, keepdims=True)
    yn = yc * jax.lax.rsqrt(var + EPS) * ln_g[...] + ln_b[...]
    conv_out = _dot((yn * _sigmoid(yn)).astype(BF16), conv_proj[...])

    q0 = 2 * CONV_DIM
    k0 = q0 + QK_W
    v0 = k0 + QK_W
    g0 = v0 + V_W
    gc0 = g0 + V_W
    gr0 = gc0 + x.shape[1]
    heads = []
    for h in range(RET_HEADS):
        qh = _dot(xn, w_in[:, q0 + h * RET_DK:q0 + (h + 1) * RET_DK])
        kh = _dot(xn, w_in[:, k0 + h * RET_DK:k0 + (h + 1) * RET_DK])
        vh = _dot(xn, w_in[:, v0 + h * RET_DV:v0 + (h + 1) * RET_DV]).astype(BF16)
        gh = _dot(xn, w_in[:, g0 + h * RET_DV:g0 + (h + 1) * RET_DV])
        qr = (qh * cos + pltpu.roll(qh, RET_DK // 2, 1) * sin).astype(BF16)
        kr = (kh * cos + pltpu.roll(kh, RET_DK // 2, 1) * sin) * (RET_DK ** -0.5)
        scores = _dot_nt(qr, kr.astype(BF16)) * dmask[h]
        inner = _dot(scores.astype(BF16), vh)
        st = state_ref[h]
        cross = _dot(qr, st.astype(BF16)) * xi[h]
        kz = (kr * zeta[h]).T.astype(BF16)
        state_ref[h] = st * _GAMMA_CHUNK[h] + _dot(kz, vh)
        o = inner + cross
        o = o * jax.lax.rsqrt(jnp.mean(o * o, axis=-1, keepdims=True) + EPS)
        heads.append((o * (gh * _sigmoid(gh))).astype(BF16))
    ret_out = _dot(jnp.concatenate(heads, axis=-1), ret_proj[...])

    gate_c = _dot(xn, w_in[:, gc0:gr0])
    gate_r = _dot(xn, w_in[:, gr0:gr0 + x.shape[1]])
    merged = _sigmoid(gate_c) * conv_out + _sigmoid(gate_r) * ret_out
    return x + _dot(merged.astype(BF16), w_out[...])


def _log_gamma():
    return np.log(1.0 - 2.0 ** (-5.0 - np.arange(RET_HEADS, dtype=np.float64)))


_GAMMA_CHUNK = [float(v) for v in np.exp(_log_gamma() * MIX_CHUNK)]


def _mixer_kernel(x_ref, meta_ref, cos_ref, sin_ref, cosm_ref, sinm_ref, *rest):
    refs = rest[:12]
    out_ref, state_ref, hbuf_ref, state0_ref, hist0_ref = rest[12:]
    b = pl.program_id(0)
    c = pl.program_id(1)

    @pl.when((b == 0) & (c == 0))
    def _():
        state_ref[...] = jnp.zeros_like(state_ref)
        hbuf_ref[0:HIST, :] = jnp.zeros((HIST, CONV_DIM), F32)
        _mix_chunk(meta_ref[...], cosm_ref[...], sinm_ref[...], refs, state_ref, hbuf_ref)
        state0_ref[...] = state_ref[...]
        hist0_ref[...] = hbuf_ref[0:HIST, :]

    @pl.when(c == 0)
    def _():
        state_ref[...] = state0_ref[...]
        hbuf_ref[0:HIST, :] = hist0_ref[...]

    out_ref[0] = _mix_chunk(x_ref[0], cos_ref[...], sin_ref[...], refs, state_ref, hbuf_ref)


def _const_spec(shape):
    nd = len(shape)
    return pl.BlockSpec(shape, lambda b, c, _nd=nd: (0,) * _nd, pipeline_mode=pl.Buffered(1))


def _mixer_call(x, meta_chunk, cos, sin, cos_m, sin_m, consts):
    B, S, D = x.shape
    C = MIX_CHUNK
    assert S % C == 0 and D % LANES == 0
    in_specs = [
        pl.BlockSpec((1, C, D), lambda b, c: (b, c, 0)),
        _const_spec(meta_chunk.shape),
        pl.BlockSpec((C, RET_DK), lambda b, c: (c, 0)),
        pl.BlockSpec((C, RET_DK), lambda b, c: (c, 0)),
        _const_spec(cos_m.shape),
        _const_spec(sin_m.shape),
    ] + [_const_spec(a.shape) for a in consts]
    return pl.pallas_call(
        _mixer_kernel,
        grid=(B, S // C),
        in_specs=in_specs,
        out_specs=pl.BlockSpec((1, C, D), lambda b, c: (b, c, 0)),
        out_shape=jax.ShapeDtypeStruct((B, S, D), F32),
        scratch_shapes=[
            pltpu.VMEM((RET_HEADS, RET_DK, RET_DV), F32),
            pltpu.VMEM((HIST + C, CONV_DIM), F32),
            pltpu.VMEM((RET_HEADS, RET_DK, RET_DV), F32),
            pltpu.VMEM((HIST, CONV_DIM), F32),
        ],
        compiler_params=pltpu.CompilerParams(
            dimension_semantics=("arbitrary", "arbitrary"), vmem_limit_bytes=VMEM_LIMIT_BYTES),
        name="token_mixer",
    )(x, meta_chunk, cos, sin, cos_m, sin_m, *consts)


def _top_sorted(s, k):
    rows = []
    for _ in range(k):
        m = jnp.max(s, axis=0, keepdims=True)
        rows.append(m)
        s = jnp.where(s == m, -jnp.inf, s)
    return rows


def _gelu_tanh(x):
    inner = x * (0.7978845608028654 + 0.035677408136300125 * (x * x))
    hx = 0.5 * x
    return hx + hx * jnp.tanh(inner)


def _peer_kernel(h_ref, gffn_ref, gfin_ref, wqt_ref, keys_ref, u_ref, vt_ref, out_ref,
                 xt_ref, sb_ref, eb_ref, th_ref, ea_ref, acc_ref, atop_ref, btop_ref, wa_ref):
    e = pl.program_id(1)
    n_e = pl.num_programs(1)
    T = h_ref.shape[0]
    K = PEER_TOPK
    NK = PEER_NKEYS
    rows_per_step = u_ref.shape[0] // NK

    @pl.when(e == 0)
    def _():
        x = h_ref[...]
        ms = jnp.mean(x * x, axis=-1, keepdims=True)
        xn = x * jax.lax.rsqrt(ms + EPS) * gffn_ref[...]
        xt_ref[...] = xn.T.astype(BF16)
        qt = _dot(wqt_ref[...], xt_ref[...]).astype(BF16)
        half = PEER_DKEY // 2
        for hp in range(2 * PEER_HEADS):
            h, p = divmod(hp, 2)
            s = _dot(keys_ref[hp], qt[hp * half:(hp + 1) * half, :])
            top = _top_sorted(s, K + 1)
            dst = atop_ref if p == 0 else btop_ref
            for r in range(K + 1):
                dst[r, h:h + 1, :] = top[r]
            if p == 0:
                th_ref[h] = s
                ea_ref[h] = jnp.exp(s - top[0])
            else:
                sb_ref[h] = s
                eb_ref[h] = jnp.exp(s - top[0])
        a = [atop_ref[r] for r in range(K + 1)]
        bt = [btop_ref[r] for r in range(K + 1)]
        cands = [a[r] + bt[s] for r in range(K + 1) for s in range(K + 1) if (r + 1) * (s + 1) <= K + 1]
        cmax = a[0] + bt[0]
        work = list(cands)
        kth = cmax
        for _ in range(K):
            kth = functools.reduce(jnp.maximum, work)
            work = [jnp.where(v == kth, -jnp.inf, v) for v in work]
        nxt = functools.reduce(jnp.maximum, work)
        tau = 0.5 * (kth + nxt)
        z = functools.reduce(
            lambda p_, q_: p_ + q_, [jnp.where(v >= tau, jnp.exp(v - cmax), 0.0) for v in cands])
        rz = 1.0 / z
        for h in range(PEER_HEADS):
            th_ref[h] = tau[h:h + 1, :] - th_ref[h]
            ea_ref[h] = ea_ref[h] * rz[h:h + 1, :]
        acc_ref[...] = jnp.zeros_like(acc_ref)

    act = _dot(u_ref[...], xt_ref[...])
    for ii in range(rows_per_step):
        i = e * rows_per_step + ii
        w = jnp.zeros((NK, T), F32)
        for h in range(PEER_HEADS):
            thr = th_ref[h, pl.ds(i, 1), :]
            sel = jnp.where(sb_ref[h] >= thr, eb_ref[h], 0.0)
            w = w + sel * ea_ref[h, pl.ds(i, 1), :]
        wa_ref[ii * NK:(ii + 1) * NK, :] = (w * _gelu_tanh(act[ii * NK:(ii + 1) * NK, :])).astype(BF16)
    acc_ref[...] += _dot(vt_ref[...], wa_ref[...])

    @pl.when(e == n_e - 1)
    def _():
        y = h_ref[...] + acc_ref[...].T
        ms = jnp.mean(y * y, axis=-1, keepdims=True)
        out_ref[...] = y * jax.lax.rsqrt(ms + EPS) * gfin_ref[...]


def _peer_call(h, g_ffn, g_fin, wqt, keys, u, vt):
    N, D = h.shape
    T = PEER_TOKENS
    E = PEER_EXPERTS
    n_exp = u.shape[0]
    assert N % T == 0 and n_exp % E == 0 and E % PEER_NKEYS == 0
    const2 = lambda t, e: (0, 0)
    return pl.pallas_call(
        _peer_kernel,
        grid=(N // T, n_exp // E),
        in_specs=[
            pl.BlockSpec((T, D), lambda t, e: (t, 0)),
            pl.BlockSpec(g_ffn.shape, const2),
            pl.BlockSpec(g_fin.shape, const2),
            pl.BlockSpec(wqt.shape, const2, pipeline_mode=pl.Buffered(1)),
            pl.BlockSpec(keys.shape, lambda t, e: (0, 0, 0), pipeline_mode=pl.Buffered(1)),
            pl.BlockSpec((E, D), lambda t, e: (e, 0)),
            pl.BlockSpec((D, E), lambda t, e: (0, e)),
        ],
        out_specs=pl.BlockSpec((T, D), lambda t, e: (t, 0)),
        out_shape=jax.ShapeDtypeStruct((N, D), F32),
        scratch_shapes=[
            pltpu.VMEM((D, T), BF16),
            pltpu.VMEM((PEER_HEADS, PEER_NKEYS, T), F32),
            pltpu.VMEM((PEER_HEADS, PEER_NKEYS, T), F32),
            pltpu.VMEM((PEER_HEADS, PEER_NKEYS, T), F32),
            pltpu.VMEM((PEER_HEADS, PEER_NKEYS, T), F32),
            pltpu.VMEM((D, T), F32),
            pltpu.VMEM((PEER_TOPK + 1, PEER_HEADS, T), F32),
            pltpu.VMEM((PEER_TOPK + 1, PEER_HEADS, T), F32),
            pltpu.VMEM((E, T), BF16),
        ],
        compiler_params=pltpu.CompilerParams(
            dimension_semantics=("arbitrary", "arbitrary"), vmem_limit_bytes=VMEM_LIMIT_BYTES),
        name="peer_dense",
    )(h, g_ffn, g_fin, wqt, keys, u, vt)


def _rotary_tables(pos):
    half = RET_DK // 2
    freqs = ROPE_BASE ** (-jnp.arange(half, dtype=F32) / half)
    ang = pos[:, None] * freqs[None, :]
    cos, sin = jnp.cos(ang), jnp.sin(ang)
    return jnp.concatenate([cos, cos], axis=-1), jnp.concatenate([-sin, sin], axis=-1)


def _decay_tables(C):
    log_gamma = jnp.log(1.0 - 2.0 ** (-5.0 - jnp.arange(RET_HEADS, dtype=F32)))
    i = jnp.arange(C, dtype=F32)
    rel = i[:, None] - i[None, :]
    dmask = jnp.where(rel >= 0, jnp.exp(log_gamma[:, None, None] * jnp.maximum(rel, 0.0)), 0.0)
    xi = jnp.exp(log_gamma[:, None] * (i[None, :] + 1.0))
    zeta = jnp.exp(log_gamma[:, None] * (C - 1.0 - i[None, :]))
    xi = jnp.broadcast_to(xi[:, :, None], (RET_HEADS, C, RET_DV))
    zeta = jnp.broadcast_to(zeta[:, :, None], (RET_HEADS, C, RET_DK))
    return dmask, xi, zeta


def kernel(x, meta_tokens, norm_mix_g, w_in, conv_dw_w, conv_dw_b, conv_ln_g, conv_ln_b, conv_proj_w, ret_proj_w, w_out, norm_ffn_g, peer_w_q, peer_sub_keys, peer_u, peer_v, norm_final_g):
    B, S, D = x.shape
    assert norm_mix_g.shape[0] == 1, "single-layer block"
    C = MIX_CHUNK

    cos, sin = _rotary_tables(jnp.arange(N_META, N_META + S, dtype=F32))
    cos_m, sin_m = _rotary_tables(jnp.arange(N_META - C, N_META, dtype=F32))
    dmask, xi, zeta = _decay_tables(C)
    meta_chunk = jnp.concatenate([jnp.zeros((C - N_META, D), x.dtype), meta_tokens.astype(x.dtype)], axis=0)

    row = lambda v: v.reshape(1, -1).astype(F32)
    consts = (
        row(norm_mix_g[0]), w_in[0].astype(BF16), conv_dw_w[0].astype(F32), row(conv_dw_b[0]),
        row(conv_ln_g[0]), row(conv_ln_b[0]), conv_proj_w[0].astype(BF16), ret_proj_w[0].astype(BF16),
        w_out[0].astype(BF16), dmask, xi, zeta,
    )
    h1 = _mixer_call(x, meta_chunk, cos, sin, cos_m, sin_m, consts)

    keys = peer_sub_keys[0].reshape(2 * PEER_HEADS, PEER_NKEYS, PEER_DKEY // 2).astype(BF16)
    y = _peer_call(
        h1.reshape(B * S, D), row(norm_ffn_g[0]), row(norm_final_g),
        peer_w_q[0].T.astype(BF16), keys, peer_u[0].astype(BF16), peer_v[0].T.astype(BF16))
    return y.reshape(B, S, D)
```

```python
import functools

import numpy as np
import jax
import jax.numpy as jnp
from jax.experimental import pallas as pl
from jax.experimental.pallas import tpu as pltpu

N_META = 16
CONV_DIM = 512
CONV_WIDTH = 31
RET_HEADS = 4
RET_DK = 128
RET_DV = 256
ROPE_BASE = 10000.0
PEER_HEADS = 8
PEER_NKEYS = 128
PEER_DKEY = 256
PEER_TOPK = 16
EPS = 1e-6
LOG2E = 1.4426950408889634

QK_W = RET_HEADS * RET_DK
V_W = RET_HEADS * RET_DV

V7X_VMEM_BYTES = 64 * 1024 * 1024
VMEM_LIMIT_BYTES = V7X_VMEM_BYTES - 8 * 1024 * 1024
LANES = 128
SUBLANES = 8

MIX_CHUNK = 256
HIST = 32
PEER_TOKENS = 512
PEER_EXPERTS = 2048
GATE_ROWS = 64
GATE_GROUP = 2

BF16 = jnp.bfloat16
F32 = jnp.float32


def _dot(a, b):
    return jnp.dot(a, b, preferred_element_type=F32)


def _dot_nt(a, b):
    return jax.lax.dot_general(a, b, (((1,), (1,)), ((), ())), preferred_element_type=F32)


def _sigmoid(x):
    return 1.0 / (1.0 + jnp.exp(-x))


def _mix_chunk(x, cos, sin, refs, state_ref, hbuf_ref):
    (g_mix, w_in, dw_w, dw_b, ln_g, ln_b, conv_proj, ret_proj, w_out, dmask, xi, zeta) = refs
    C = x.shape[0]

    ms = jnp.mean(x * x, axis=-1, keepdims=True)
    xn = (x * jax.lax.rsqrt(ms + EPS) * g_mix[...]).astype(BF16)

    a = _dot(xn, w_in[:, 0:CONV_DIM])
    b = _dot(xn, w_in[:, CONV_DIM:2 * CONV_DIM])
    hbuf_ref[HIST:HIST + C, :] = a * _sigmoid(b)
    y = jnp.zeros((C, CONV_DIM), F32) + dw_b[...]
    for w in range(CONV_WIDTH):
        off = HIST - (CONV_WIDTH - 1) + w
        y = y + hbuf_ref[off:off + C, :] * dw_w[w:w + 1, :]
    hbuf_ref[0:HIST, :] = hbuf_ref[C:C + HIST, :]
    mu = jnp.mean(y, axis=-1, keepdims=True)
    yc = y - mu
    var = jnp.mean(yc * yc, axis=-1, keepdims=True)
    yn = yc * jax.lax.rsqrt(var + EPS) * ln_g[...] + ln_b[...]
    conv_out = _dot((yn * _sigmoid(yn)).astype(BF16), conv_proj[...])

    q0 = 2 * CONV_DIM
    k0 = q0 + QK_W
    v0 = k0 + QK_W
    g0 = v0 + V_W
    gc0 = g0 + V_W
    gr0 = gc0 + x.shape[1]
    heads = []
    for h in range(RET_HEADS):
        qh = _dot(xn, w_in[:, q0 + h * RET_DK:q0 + (h + 1) * RET_DK])
        kh = _dot(xn, w_in[:, k0 + h * RET_DK:k0 + (h + 1) * RET_DK])
        vh = _dot(xn, w_in[:, v0 + h * RET_DV:v0 + (h + 1) * RET_DV]).astype(BF16)
        gh = _dot(xn, w_in[:, g0 + h * RET_DV:g0 + (h + 1) * RET_DV])
        qr = (qh * cos + pltpu.roll(qh, RET_DK // 2, 1) * sin).astype(BF16)
        kr = (kh * cos + pltpu.roll(kh, RET_DK // 2, 1) * sin) * (RET_DK ** -0.5)
        scores = _dot_nt(qr, kr.astype(BF16)) * dmask[h]
        inner = _dot(scores.astype(BF16), vh)
        st = state_ref[h]
        cross = _dot(qr, st.astype(BF16)) * xi[h]
        kz = (kr * zeta[h]).T.astype(BF16)
        state_ref[h] = st * _GAMMA_CHUNK[h] + _dot(kz, vh)
        o = inner + cross
        o = o * jax.lax.rsqrt(jnp.mean(o * o, axis=-1, keepdims=True) + EPS)
        heads.append((o * (gh * _sigmoid(gh))).astype(BF16))
    ret_out = _dot(jnp.concatenate(heads, axis=-1), ret_proj[...])

    gate_c = _dot(xn, w_in[:, gc0:gr0])
    gate_r = _dot(xn, w_in[:, gr0:gr0 + x.shape[1]])
    merged = _sigmoid(gate_c) * conv_out + _sigmoid(gate_r) * ret_out
    return x + _dot(merged.astype(BF16), w_out[...])


def _log_gamma():
    return np.log(1.0 - 2.0 ** (-5.0 - np.arange(RET_HEADS, dtype=np.float64)))


_GAMMA_CHUNK = [float(v) for v in np.exp(_log_gamma() * MIX_CHUNK)]


def _mixer_kernel(x_ref, meta_ref, cos_ref, sin_ref, cosm_ref, sinm_ref, *rest):
    refs = rest[:12]
    out_ref, state_ref, hbuf_ref, state0_ref, hist0_ref = rest[12:]
    b = pl.program_id(0)
    c = pl.program_id(1)

    @pl.when((b == 0) & (c == 0))
    def _():
        state_ref[...] = jnp.zeros_like(state_ref)
        hbuf_ref[0:HIST, :] = jnp.zeros((HIST, CONV_DIM), F32)
        _mix_chunk(meta_ref[...], cosm_ref[...], sinm_ref[...], refs, state_ref, hbuf_ref)
        state0_ref[...] = state_ref[...]
        hist0_ref[...] = hbuf_ref[0:HIST, :]

    @pl.when(c == 0)
    def _():
        state_ref[...] = state0_ref[...]
        hbuf_ref[0:HIST, :] = hist0_ref[...]

    out_ref[0] = _mix_chunk(x_ref[0], cos_ref[...], sin_ref[...], refs, state_ref, hbuf_ref)


def _const_spec(shape):
    nd = len(shape)
    return pl.BlockSpec(shape, lambda b, c, _nd=nd: (0,) * _nd, pipeline_mode=pl.Buffered(1))


def _mixer_call(x, meta_chunk, cos, sin, cos_m, sin_m, consts):
    B, S, D = x.shape
    C = MIX_CHUNK
    assert S % C == 0 and D % LANES == 0
    in_specs = [
        pl.BlockSpec((1, C, D), lambda b, c: (b, c, 0)),
        _const_spec(meta_chunk.shape),
        pl.BlockSpec((C, RET_DK), lambda b, c: (c, 0)),
        pl.BlockSpec((C, RET_DK), lambda b, c: (c, 0)),
        _const_spec(cos_m.shape),
        _const_spec(sin_m.shape),
    ] + [_const_spec(a.shape) for a in consts]
    return pl.pallas_call(
        _mixer_kernel,
        grid=(B, S // C),
        in_specs=in_specs,
        out_specs=pl.BlockSpec((1, C, D), lambda b, c: (b, c, 0)),
        out_shape=jax.ShapeDtypeStruct((B, S, D), F32),
        scratch_shapes=[
            pltpu.VMEM((RET_HEADS, RET_DK, RET_DV), F32),
            pltpu.VMEM((HIST + C, CONV_DIM), F32),
            pltpu.VMEM((RET_HEADS, RET_DK, RET_DV), F32),
            pltpu.VMEM((HIST, CONV_DIM), F32),
        ],
        compiler_params=pltpu.CompilerParams(
            dimension_semantics=("arbitrary", "arbitrary"), vmem_limit_bytes=VMEM_LIMIT_BYTES),
        name="token_mixer",
    )(x, meta_chunk, cos, sin, cos_m, sin_m, *consts)


def _batcher_pairs(n):
    pairs = []
    p = 1
    while p < n:
        k = p
        while k >= 1:
            for j in range(k % p, n - k, 2 * k):
                for i in range(min(k, n - j - k)):
                    if (i + j) // (2 * p) == (i + j + k) // (2 * p):
                        pairs.append((i + j, i + j + k))
            k //= 2
        p *= 2
    return pairs


def _sort_desc(v, pairs):
    v = list(v)
    for i, j in pairs:
        v[i], v[j] = jnp.maximum(v[i], v[j]), jnp.minimum(v[i], v[j])
    return v


def _top_sorted(s, k):
    n, T = s.shape
    assert n == k * SUBLANES and k & (k - 1) == 0
    v = _sort_desc([s[i * SUBLANES:(i + 1) * SUBLANES, :] for i in range(k)], _batcher_pairs(k))
    bitonic = [(i, i + d) for d in (k >> b for b in range(1, k.bit_length())) for i in range(k) if not i & d]
    shift = SUBLANES // 2
    while shift:
        v = _sort_desc([jnp.maximum(v[i], pltpu.roll(v[k - 1 - i], shift, 0)) for i in range(k)], bitonic)
        shift //= 2
    rows = [x[0:1, :] for x in v]
    rows.append(jnp.max(jnp.where(s < rows[-1], s, -jnp.inf), axis=0, keepdims=True))
    return rows


def _gelu_tanh(x):
    inner = x * (0.7978845608028654 + 0.035677408136300125 * (x * x))
    hx = 0.5 * x
    return hx + hx * jnp.tanh(inner)


def _peer_kernel(h_ref, gffn_ref, gfin_ref, wqt_ref, keys_ref, u_ref, vt_ref, out_ref,
                 xt_ref, sb_ref, th_ref, ea_ref, acc_ref, atop_ref, btop_ref):
    s = pl.program_id(1)
    T = h_ref.shape[0]
    K = PEER_TOPK
    NK = PEER_NKEYS
    rows_per_step = u_ref.shape[0] // NK

    @pl.when(s == 0)
    def _():
        x = h_ref[...]
        ms = jnp.mean(x * x, axis=-1, keepdims=True)
        xn = x * jax.lax.rsqrt(ms + EPS) * gffn_ref[...]
        xt_ref[...] = xn.T.astype(BF16)
        qt = _dot(wqt_ref[...], xt_ref[...]).astype(BF16)
        half = PEER_DKEY // 2
        for hp in range(2 * PEER_HEADS):
            h, p = divmod(hp, 2)
            sc = _dot(keys_ref[hp], qt[hp * half:(hp + 1) * half, :])
            top = _top_sorted(sc, K)
            dst = atop_ref if p == 0 else btop_ref
            for r in range(K + 1):
                dst[r, h:h + 1, :] = top[r]
            if p == 0:
                ea = jnp.exp(sc - top[0])
                for lt in range(T // LANES):
                    th_ref[h, lt] = sc[:, lt * LANES:(lt + 1) * LANES]
                    ea_ref[h, lt] = ea[:, lt * LANES:(lt + 1) * LANES]
            else:
                sb_ref[h] = (sc - top[0]) * LOG2E
        a = [atop_ref[r] for r in range(K + 1)]
        bt = [btop_ref[r] for r in range(K + 1)]
        cands = [a[r] + bt[c] for r in range(K + 1) for c in range(K + 1) if (r + 1) * (c + 1) <= K + 1]
        cmax = a[0] + bt[0]
        work = list(cands)
        kth = cmax
        for _ in range(K):
            kth = functools.reduce(jnp.maximum, work)
            work = [jnp.where(v == kth, -jnp.inf, v) for v in work]
        nxt = functools.reduce(jnp.maximum, work)
        tau = 0.5 * (kth + nxt)
        z = functools.reduce(
            lambda p_, q_: p_ + q_, [jnp.where(v >= tau, jnp.exp(v - cmax), 0.0) for v in cands])
        rz = 1.0 / z
        for h in range(PEER_HEADS):
            for lt in range(T // LANES):
                lanes = slice(lt * LANES, (lt + 1) * LANES)
                th_ref[h, lt] = ((tau - bt[0])[h:h + 1, lanes] - th_ref[h, lt]) * LOG2E
                ea_ref[h, lt] = ea_ref[h, lt] * rz[h:h + 1, lanes]
        acc_ref[...] = jnp.zeros_like(acc_ref)

    chunk = GATE_GROUP * NK
    n_chunks = rows_per_step // GATE_GROUP
    scores = lambda g: _dot(u_ref[g * chunk:(g + 1) * chunk, :], xt_ref[...])
    total = None
    nxt = scores(0)
    for g in range(n_chunks):
        act = nxt
        if g + 1 < n_chunks:
            nxt = scores(g + 1)
        cols = []
        for lt in range(T // LANES):
            lanes = slice(lt * LANES, (lt + 1) * LANES)
            tiles = [[None] * (NK // GATE_ROWS) for _ in range(GATE_GROUP)]
            i0 = s * rows_per_step + g * GATE_GROUP
            thr = [[th_ref[h, lt, pl.ds(i0 + r, 1), :] for h in range(PEER_HEADS)] for r in range(GATE_GROUP)]
            eav = [[ea_ref[h, lt, pl.ds(i0 + r, 1), :] for h in range(PEER_HEADS)] for r in range(GATE_GROUP)]
            for rt in range(NK // GATE_ROWS):
                rows = slice(rt * GATE_ROWS, (rt + 1) * GATE_ROWS)
                accs = [None] * GATE_GROUP
                for h in range(PEER_HEADS):
                    sbt = sb_ref[h, rows, lanes]
                    ebt = jnp.exp2(sbt)
                    for r in range(GATE_GROUP):
                        term = jnp.where(sbt >= thr[r][h], ebt, 0.0) * eav[r][h]
                        accs[r] = term if accs[r] is None else accs[r] + term
                for r in range(GATE_GROUP):
                    r0 = r * NK + rt * GATE_ROWS
                    tiles[r][rt] = (accs[r] * _gelu_tanh(act[r0:r0 + GATE_ROWS, lanes])).astype(BF16)
            cols.append(jnp.concatenate([t for per_key in tiles for t in per_key], axis=0))
        wa = jnp.concatenate(cols, axis=1)
        part = _dot(vt_ref[:, g * chunk:(g + 1) * chunk], wa)
        total = part if total is None else total + part
    acc_ref[...] += total

    @pl.when(s == pl.num_programs(1) - 1)
    def _():
        y = h_ref[...] + acc_ref[...].T
        ms = jnp.mean(y * y, axis=-1, keepdims=True)
        out_ref[...] = y * jax.lax.rsqrt(ms + EPS) * gfin_ref[...]


def _peer_call(h, g_ffn, g_fin, wqt, keys, u, vt):
    N, D = h.shape
    T = PEER_TOKENS
    E = PEER_EXPERTS
    n_exp = u.shape[0]
    assert N % T == 0 and n_exp % E == 0 and E % PEER_NKEYS == 0
    const2 = lambda t, s: (0, 0)
    return pl.pallas_call(
        _peer_kernel,
        grid=(N // T, n_exp // E),
        in_specs=[
            pl.BlockSpec((T, D), lambda t, s: (t, 0)),
            pl.BlockSpec(g_ffn.shape, const2),
            pl.BlockSpec(g_fin.shape, const2),
            pl.BlockSpec(wqt.shape, const2, pipeline_mode=pl.Buffered(1)),
            pl.BlockSpec(keys.shape, lambda t, s: (0, 0, 0), pipeline_mode=pl.Buffered(1)),
            pl.BlockSpec((E, D), lambda t, s: (s, 0)),
            pl.BlockSpec((D, E), lambda t, s: (0, s)),
        ],
        out_specs=pl.BlockSpec((T, D), lambda t, s: (t, 0)),
        out_shape=jax.ShapeDtypeStruct((N, D), F32),
        scratch_shapes=[
            pltpu.VMEM((D, T), BF16),
            pltpu.VMEM((PEER_HEADS, PEER_NKEYS, T), F32),
            pltpu.VMEM((PEER_HEADS, T // LANES, PEER_NKEYS, LANES), F32),
            pltpu.VMEM((PEER_HEADS, T // LANES, PEER_NKEYS, LANES), F32),
            pltpu.VMEM((D, T), F32),
            pltpu.VMEM((PEER_TOPK + 1, PEER_HEADS, T), F32),
            pltpu.VMEM((PEER_TOPK + 1, PEER_HEADS, T), F32),
        ],
        compiler_params=pltpu.CompilerParams(
            dimension_semantics=("arbitrary", "arbitrary"), vmem_limit_bytes=VMEM_LIMIT_BYTES),
        name="peer_dense",
    )(h, g_ffn, g_fin, wqt, keys, u, vt)


def _rotary_tables(pos):
    half = RET_DK // 2
    freqs = ROPE_BASE ** (-jnp.arange(half, dtype=F32) / half)
    ang = pos[:, None] * freqs[None, :]
    cos, sin = jnp.cos(ang), jnp.sin(ang)
    return jnp.concatenate([cos, cos], axis=-1), jnp.concatenate([-sin, sin], axis=-1)


def _decay_tables(C):
    log_gamma = jnp.log(1.0 - 2.0 ** (-5.0 - jnp.arange(RET_HEADS, dtype=F32)))
    i = jnp.arange(C, dtype=F32)
    rel = i[:, None] - i[None, :]
    dmask = jnp.where(rel >= 0, jnp.exp(log_gamma[:, None, None] * jnp.maximum(rel, 0.0)), 0.0)
    xi = jnp.exp(log_gamma[:, None] * (i[None, :] + 1.0))
    zeta = jnp.exp(log_gamma[:, None] * (C - 1.0 - i[None, :]))
    xi = jnp.broadcast_to(xi[:, :, None], (RET_HEADS, C, RET_DV))
    zeta = jnp.broadcast_to(zeta[:, :, None], (RET_HEADS, C, RET_DK))
    return dmask, xi, zeta


def kernel(x, meta_tokens, norm_mix_g, w_in, conv_dw_w, conv_dw_b, conv_ln_g, conv_ln_b, conv_proj_w, ret_proj_w, w_out, norm_ffn_g, peer_w_q, peer_sub_keys, peer_u, peer_v, norm_final_g):
    B, S, D = x.shape
    assert norm_mix_g.shape[0] == 1, "single-layer block"
    C = MIX_CHUNK

    cos, sin = _rotary_tables(jnp.arange(N_META, N_META + S, dtype=F32))
    cos_m, sin_m = _rotary_tables(jnp.arange(N_META - C, N_META, dtype=F32))
    dmask, xi, zeta = _decay_tables(C)
    meta_chunk = jnp.concatenate([jnp.zeros((C - N_META, D), x.dtype), meta_tokens.astype(x.dtype)], axis=0)

    row = lambda v: v.reshape(1, -1).astype(F32)
    consts = (
        row(norm_mix_g[0]), w_in[0].astype(BF16), conv_dw_w[0].astype(F32), row(conv_dw_b[0]),
        row(conv_ln_g[0]), row(conv_ln_b[0]), conv_proj_w[0].astype(BF16), ret_proj_w[0].astype(BF16),
        w_out[0].astype(BF16), dmask, xi, zeta,
    )
    h1 = _mixer_call(x, meta_chunk, cos, sin, cos_m, sin_m, consts)

    keys = peer_sub_keys[0].reshape(2 * PEER_HEADS, PEER_NKEYS, PEER_DKEY // 2).astype(BF16)
    y = _peer_call(
        h1.reshape(B * S, D), row(norm_ffn_g[0]), row(norm_final_g),
        peer_w_q[0].T.astype(BF16), keys, peer_u[0].astype(BF16), peer_v[0].T.astype(BF16))
    return y.reshape(B, S, D)
```

```python
import functools

import numpy as np
import jax
import jax.numpy as jnp
from jax.experimental import pallas as pl
from jax.experimental.pallas import tpu as pltpu

N_META = 16
CONV_DIM = 512
CONV_WIDTH = 31
RET_HEADS = 4
RET_DK = 128
RET_DV = 256
ROPE_BASE = 10000.0
PEER_HEADS = 8
PEER_NKEYS = 128
PEER_DKEY = 256
PEER_TOPK = 16
EPS = 1e-6
LOG2E = 1.4426950408889634

QK_W = RET_HEADS * RET_DK
V_W = RET_HEADS * RET_DV

V7X_VMEM_BYTES = 64 * 1024 * 1024
VMEM_LIMIT_BYTES = V7X_VMEM_BYTES - 8 * 1024 * 1024
LANES = 128
SUBLANES = 8

MIX_CHUNK = 512
HIST = 32
PEER_TOKENS = 512
PEER_EXPERTS = 2048
GATE_ROWS = 32
GATE_GROUP = 4

BF16 = jnp.bfloat16
F32 = jnp.float32


def _dot(a, b):
    return jnp.dot(a, b, preferred_element_type=F32)


def _dot_nt(a, b):
    return jax.lax.dot_general(a, b, (((1,), (1,)), ((), ())), preferred_element_type=F32)


def _sigmoid(x):
    return 1.0 / (1.0 + jnp.exp(-x))


def _mix_chunk(x, cos, sin, refs, state_ref, hbuf_ref):
    (g_mix, w_in, dw_w, dw_b, ln_g, ln_b, conv_proj, ret_proj, w_out, dmask, xi, zeta) = refs
    C = x.shape[0]

    ms = jnp.mean(x * x, axis=-1, keepdims=True)
    xn = (x * jax.lax.rsqrt(ms + EPS) * g_mix[...]).astype(BF16)

    q0 = 2 * CONV_DIM
    k0 = q0 + QK_W
    v0 = k0 + QK_W
    g0 = v0 + V_W
    gc0 = g0 + V_W
    gr0 = gc0 + x.shape[1]
    ab = _dot(xn, w_in[:, 0:q0])
    qk = _dot(xn, w_in[:, q0:v0])
    v_all = _dot(xn, w_in[:, v0:g0]).astype(BF16)
    g_all = _dot(xn, w_in[:, g0:gc0])
    gates = _dot(xn, w_in[:, gc0:gr0 + x.shape[1]])

    hbuf_ref[HIST:HIST + C, :] = ab[:, 0:CONV_DIM] * _sigmoid(ab[:, CONV_DIM:q0])
    y = jnp.zeros((C, CONV_DIM), F32) + dw_b[...]
    for w in range(CONV_WIDTH):
        off = HIST - (CONV_WIDTH - 1) + w
        y = y + hbuf_ref[off:off + C, :] * dw_w[w:w + 1, :]
    hbuf_ref[0:HIST, :] = hbuf_ref[C:C + HIST, :]
    mu = jnp.mean(y, axis=-1, keepdims=True)
    yc = y - mu
    var = jnp.mean(yc * yc, axis=-1, keepdims=True)
    yn = yc * jax.lax.rsqrt(var + EPS) * ln_g[...] + ln_b[...]
    conv_out = _dot((yn * _sigmoid(yn)).astype(BF16), conv_proj[...])

    heads = []
    for h in range(RET_HEADS):
        qh = qk[:, h * RET_DK:(h + 1) * RET_DK]
        kh = qk[:, QK_W + h * RET_DK:QK_W + (h + 1) * RET_DK]
        vh = v_all[:, h * RET_DV:(h + 1) * RET_DV]
        gh = g_all[:, h * RET_DV:(h + 1) * RET_DV]
        qr = (qh * cos + pltpu.roll(qh, RET_DK // 2, 1) * sin).astype(BF16)
        kr = (kh * cos + pltpu.roll(kh, RET_DK // 2, 1) * sin) * (RET_DK ** -0.5)
        scores = _dot_nt(qr, kr.astype(BF16)) * dmask[h]
        inner = _dot(scores.astype(BF16), vh)
        st = state_ref[h]
        cross = _dot(qr, st.astype(BF16)) * xi[h]
        kz = (kr * zeta[h]).T.astype(BF16)
        state_ref[h] = st * _GAMMA_CHUNK[h] + _dot(kz, vh)
        o = inner + cross
        o = o * jax.lax.rsqrt(jnp.mean(o * o, axis=-1, keepdims=True) + EPS)
        heads.append((o * (gh * _sigmoid(gh))).astype(BF16))
    ret_out = _dot(jnp.concatenate(heads, axis=-1), ret_proj[...])

    gate_c = gates[:, 0:x.shape[1]]
    gate_r = gates[:, x.shape[1]:]
    merged = _sigmoid(gate_c) * conv_out + _sigmoid(gate_r) * ret_out
    return x + _dot(merged.astype(BF16), w_out[...])


def _log_gamma():
    return np.log(1.0 - 2.0 ** (-5.0 - np.arange(RET_HEADS, dtype=np.float64)))


_GAMMA_CHUNK = [float(v) for v in np.exp(_log_gamma() * MIX_CHUNK)]


def _mixer_kernel(x_ref, meta_ref, cos_ref, sin_ref, cosm_ref, sinm_ref, *rest):
    refs = rest[:12]
    out_ref, state_ref, hbuf_ref, state0_ref, hist0_ref = rest[12:]
    b = pl.program_id(0)
    c = pl.program_id(1)

    @pl.when((b == 0) & (c == 0))
    def _():
        state_ref[...] = jnp.zeros_like(state_ref)
        hbuf_ref[0:HIST, :] = jnp.zeros((HIST, CONV_DIM), F32)
        _mix_chunk(meta_ref[...], cosm_ref[...], sinm_ref[...], refs, state_ref, hbuf_ref)
        state0_ref[...] = state_ref[...]
        hist0_ref[...] = hbuf_ref[0:HIST, :]

    @pl.when(c == 0)
    def _():
        state_ref[...] = state0_ref[...]
        hbuf_ref[0:HIST, :] = hist0_ref[...]

    out_ref[0] = _mix_chunk(x_ref[0], cos_ref[...], sin_ref[...], refs, state_ref, hbuf_ref)


def _const_spec(shape):
    nd = len(shape)
    return pl.BlockSpec(shape, lambda b, c, _nd=nd: (0,) * _nd, pipeline_mode=pl.Buffered(1))


def _mixer_call(x, meta_chunk, cos, sin, cos_m, sin_m, consts):
    B, S, D = x.shape
    C = MIX_CHUNK
    assert S % C == 0 and D % LANES == 0
    in_specs = [
        pl.BlockSpec((1, C, D), lambda b, c: (b, c, 0)),
        _const_spec(meta_chunk.shape),
        pl.BlockSpec((C, RET_DK), lambda b, c: (c, 0)),
        pl.BlockSpec((C, RET_DK), lambda b, c: (c, 0)),
        _const_spec(cos_m.shape),
        _const_spec(sin_m.shape),
    ] + [_const_spec(a.shape) for a in consts]
    return pl.pallas_call(
        _mixer_kernel,
        grid=(B, S // C),
        in_specs=in_specs,
        out_specs=pl.BlockSpec((1, C, D), lambda b, c: (b, c, 0)),
        out_shape=jax.ShapeDtypeStruct((B, S, D), F32),
        scratch_shapes=[
            pltpu.VMEM((RET_HEADS, RET_DK, RET_DV), F32),
            pltpu.VMEM((HIST + C, CONV_DIM), F32),
            pltpu.VMEM((RET_HEADS, RET_DK, RET_DV), F32),
            pltpu.VMEM((HIST, CONV_DIM), F32),
        ],
        compiler_params=pltpu.CompilerParams(
            dimension_semantics=("arbitrary", "arbitrary"), vmem_limit_bytes=VMEM_LIMIT_BYTES),
        name="token_mixer",
    )(x, meta_chunk, cos, sin, cos_m, sin_m, *consts)


def _batcher_pairs(n):
    pairs = []
    p = 1
    while p < n:
        k = p
        while k >= 1:
            for j in range(k % p, n - k, 2 * k):
                for i in range(min(k, n - j - k)):
                    if (i + j) // (2 * p) == (i + j + k) // (2 * p):
                        pairs.append((i + j, i + j + k))
            k //= 2
        p *= 2
    return pairs


def _sort_desc(v, pairs):
    v = list(v)
    for i, j in pairs:
        v[i], v[j] = jnp.maximum(v[i], v[j]), jnp.minimum(v[i], v[j])
    return v


def _top_sorted(s, k):
    n, T = s.shape
    assert n == k * SUBLANES and k & (k - 1) == 0
    v = _sort_desc([s[i * SUBLANES:(i + 1) * SUBLANES, :] for i in range(k)], _batcher_pairs(k))
    bitonic = [(i, i + d) for d in (k >> b for b in range(1, k.bit_length())) for i in range(k) if not i & d]
    shift = SUBLANES // 2
    while shift:
        v = _sort_desc([jnp.maximum(v[i], pltpu.roll(v[k - 1 - i], shift, 0)) for i in range(k)], bitonic)
        shift //= 2
    rows = [x[0:1, :] for x in v]
    rows.append(jnp.max(jnp.where(s < rows[-1], s, -jnp.inf), axis=0, keepdims=True))
    return rows


def _gelu_tanh(x):
    inner = x * (0.7978845608028654 + 0.035677408136300125 * (x * x))
    hx = 0.5 * x
    return hx + hx * jnp.tanh(inner)


def _peer_kernel(h_ref, gffn_ref, gfin_ref, wqt_ref, keys_ref, u_ref, vt_ref, out_ref,
                 xt_ref, sb_ref, th_ref, ea_ref, acc_ref, atop_ref, btop_ref):
    s = pl.program_id(1)
    T = h_ref.shape[0]
    K = PEER_TOPK
    NK = PEER_NKEYS
    rows_per_step = u_ref.shape[0] // NK

    @pl.when(s == 0)
    def _():
        x = h_ref[...]
        ms = jnp.mean(x * x, axis=-1, keepdims=True)
        xn = x * jax.lax.rsqrt(ms + EPS) * gffn_ref[...]
        xt_ref[...] = xn.T.astype(BF16)
        qt = _dot(wqt_ref[...], xt_ref[...]).astype(BF16)
        half = PEER_DKEY // 2
        for hp in range(2 * PEER_HEADS):
            h, p = divmod(hp, 2)
            sc = _dot(keys_ref[hp], qt[hp * half:(hp + 1) * half, :])
            top = _top_sorted(sc, K)
            dst = atop_ref if p == 0 else btop_ref
            for r in range(K + 1):
                dst[r, h:h + 1, :] = top[r]
            if p == 0:
                ea = jnp.exp(sc - top[0])
                for lt in range(T // LANES):
                    th_ref[h, lt] = sc[:, lt * LANES:(lt + 1) * LANES]
                    ea_ref[h, lt] = ea[:, lt * LANES:(lt + 1) * LANES]
            else:
                sb_ref[h] = (sc - top[0]) * LOG2E
        a = [atop_ref[r] for r in range(K + 1)]
        bt = [btop_ref[r] for r in range(K + 1)]
        cands = [a[r] + bt[c] for r in range(K + 1) for c in range(K + 1) if (r + 1) * (c + 1) <= K + 1]
        cmax = a[0] + bt[0]
        work = list(cands)
        kth = cmax
        for _ in range(K):
            kth = functools.reduce(jnp.maximum, work)
            work = [jnp.where(v == kth, -jnp.inf, v) for v in work]
        nxt = functools.reduce(jnp.maximum, work)
        tau = 0.5 * (kth + nxt)
        z = functools.reduce(
            lambda p_, q_: p_ + q_, [jnp.where(v >= tau, jnp.exp(v - cmax), 0.0) for v in cands])
        rz = 1.0 / z
        for h in range(PEER_HEADS):
            for lt in range(T // LANES):
                lanes = slice(lt * LANES, (lt + 1) * LANES)
                th_ref[h, lt] = ((tau - bt[0])[h:h + 1, lanes] - th_ref[h, lt]) * LOG2E
                ea_ref[h, lt] = ea_ref[h, lt] * rz[h:h + 1, lanes]
        acc_ref[...] = jnp.zeros_like(acc_ref)

    chunk = GATE_GROUP * NK
    n_chunks = rows_per_step // GATE_GROUP
    scores = lambda g: _dot(u_ref[g * chunk:(g + 1) * chunk, :], xt_ref[...])
    total = None
    nxt = scores(0)
    for g in range(n_chunks):
        act = nxt
        if g + 1 < n_chunks:
            nxt = scores(g + 1)
        cols = []
        for lt in range(T // LANES):
            lanes = slice(lt * LANES, (lt + 1) * LANES)
            tiles = [[None] * (NK // GATE_ROWS) for _ in range(GATE_GROUP)]
            i0 = s * rows_per_step + g * GATE_GROUP
            thr = [[th_ref[h, lt, pl.ds(i0 + r, 1), :] for h in range(PEER_HEADS)] for r in range(GATE_GROUP)]
            eav = [[ea_ref[h, lt, pl.ds(i0 + r, 1), :] for h in range(PEER_HEADS)] for r in range(GATE_GROUP)]
            for rt in range(NK // GATE_ROWS):
                rows = slice(rt * GATE_ROWS, (rt + 1) * GATE_ROWS)
                accs = [jnp.zeros((GATE_ROWS, LANES), F32)] * GATE_GROUP
                for h in range(PEER_HEADS):
                    sbt = sb_ref[h, rows, lanes]
                    ebt = jnp.exp2(sbt)
                    for r in range(GATE_GROUP):
                        accs[r] = jnp.where(sbt >= thr[r][h], accs[r] + ebt * eav[r][h], accs[r])
                for r in range(GATE_GROUP):
                    r0 = r * NK + rt * GATE_ROWS
                    tiles[r][rt] = (accs[r] * _gelu_tanh(act[r0:r0 + GATE_ROWS, lanes])).astype(BF16)
            cols.append(jnp.concatenate([t for per_key in tiles for t in per_key], axis=0))
        wa = jnp.concatenate(cols, axis=1)
        part = _dot(vt_ref[:, g * chunk:(g + 1) * chunk], wa)
        total = part if total is None else total + part
    acc_ref[...] += total

    @pl.when(s == pl.num_programs(1) - 1)
    def _():
        y = h_ref[...] + acc_ref[...].T
        ms = jnp.mean(y * y, axis=-1, keepdims=True)
        out_ref[...] = y * jax.lax.rsqrt(ms + EPS) * gfin_ref[...]


def _peer_call(h, g_ffn, g_fin, wqt, keys, u, vt):
    N, D = h.shape
    T = PEER_TOKENS
    E = PEER_EXPERTS
    n_exp = u.shape[0]
    assert N % T == 0 and n_exp % E == 0 and E % PEER_NKEYS == 0
    const2 = lambda t, s: (0, 0)
    return pl.pallas_call(
        _peer_kernel,
        grid=(N // T, n_exp // E),
        in_specs=[
            pl.BlockSpec((T, D), lambda t, s: (t, 0)),
            pl.BlockSpec(g_ffn.shape, const2),
            pl.BlockSpec(g_fin.shape, const2),
            pl.BlockSpec(wqt.shape, const2, pipeline_mode=pl.Buffered(1)),
            pl.BlockSpec(keys.shape, lambda t, s: (0, 0, 0), pipeline_mode=pl.Buffered(1)),
            pl.BlockSpec((E, D), lambda t, s: (s, 0)),
            pl.BlockSpec((D, E), lambda t, s: (0, s)),
        ],
        out_specs=pl.BlockSpec((T, D), lambda t, s: (t, 0)),
        out_shape=jax.ShapeDtypeStruct((N, D), F32),
        scratch_shapes=[
            pltpu.VMEM((D, T), BF16),
            pltpu.VMEM((PEER_HEADS, PEER_NKEYS, T), F32),
            pltpu.VMEM((PEER_HEADS, T // LANES, PEER_NKEYS, LANES), F32),
            pltpu.VMEM((PEER_HEADS, T // LANES, PEER_NKEYS, LANES), F32),
            pltpu.VMEM((D, T), F32),
            pltpu.VMEM((PEER_TOPK + 1, PEER_HEADS, T), F32),
            pltpu.VMEM((PEER_TOPK + 1, PEER_HEADS, T), F32),
        ],
        compiler_params=pltpu.CompilerParams(
            dimension_semantics=("arbitrary", "arbitrary"), vmem_limit_bytes=VMEM_LIMIT_BYTES),
        name="peer_dense",
    )(h, g_ffn, g_fin, wqt, keys, u, vt)


def _rotary_tables(pos):
    half = RET_DK // 2
    freqs = ROPE_BASE ** (-jnp.arange(half, dtype=F32) / half)
    ang = pos[:, None] * freqs[None, :]
    cos, sin = jnp.cos(ang), jnp.sin(ang)
    return jnp.concatenate([cos, cos], axis=-1), jnp.concatenate([-sin, sin], axis=-1)


def _decay_tables(C):
    log_gamma = jnp.log(1.0 - 2.0 ** (-5.0 - jnp.arange(RET_HEADS, dtype=F32)))
    i = jnp.arange(C, dtype=F32)
    rel = i[:, None] - i[None, :]
    dmask = jnp.where(rel >= 0, jnp.exp(log_gamma[:, None, None] * jnp.maximum(rel, 0.0)), 0.0)
    xi = jnp.exp(log_gamma[:, None] * (i[None, :] + 1.0))
    zeta = jnp.exp(log_gamma[:, None] * (C - 1.0 - i[None, :]))
    xi = jnp.broadcast_to(xi[:, :, None], (RET_HEADS, C, RET_DV))
    zeta = jnp.broadcast_to(zeta[:, :, None], (RET_HEADS, C, RET_DK))
    return dmask, xi, zeta


def kernel(x, meta_tokens, norm_mix_g, w_in, conv_dw_w, conv_dw_b, conv_ln_g, conv_ln_b, conv_proj_w, ret_proj_w, w_out, norm_ffn_g, peer_w_q, peer_sub_keys, peer_u, peer_v, norm_final_g):
    B, S, D = x.shape
    assert norm_mix_g.shape[0] == 1, "single-layer block"
    C = MIX_CHUNK

    cos, sin = _rotary_tables(jnp.arange(N_META, N_META + S, dtype=F32))
    cos_m, sin_m = _rotary_tables(jnp.arange(N_META - C, N_META, dtype=F32))
    dmask, xi, zeta = _decay_tables(C)
    meta_chunk = jnp.concatenate([jnp.zeros((C - N_META, D), x.dtype), meta_tokens.astype(x.dtype)], axis=0)

    row = lambda v: v.reshape(1, -1).astype(F32)
    consts = (
        row(norm_mix_g[0]), w_in[0].astype(BF16), conv_dw_w[0].astype(F32), row(conv_dw_b[0]),
        row(conv_ln_g[0]), row(conv_ln_b[0]), conv_proj_w[0].astype(BF16), ret_proj_w[0].astype(BF16),
        w_out[0].astype(BF16), dmask, xi, zeta,
    )
    h1 = _mixer_call(x, meta_chunk, cos, sin, cos_m, sin_m, consts)

    keys = peer_sub_keys[0].reshape(2 * PEER_HEADS, PEER_NKEYS, PEER_DKEY // 2).astype(BF16)
    y = _peer_call(
        h1.reshape(B * S, D), row(norm_ffn_g[0]), row(norm_final_g),
        peer_w_q[0].T.astype(BF16), keys, peer_u[0].astype(BF16), peer_v[0].T.astype(BF16))
    return y.reshape(B, S, D)
```

```python
import functools

import numpy as np
import jax
import jax.numpy as jnp
from jax.experimental import pallas as pl
from jax.experimental.pallas import tpu as pltpu

N_META = 16
CONV_DIM = 512
CONV_WIDTH = 31
RET_HEADS = 4
RET_DK = 128
RET_DV = 256
ROPE_BASE = 10000.0
PEER_HEADS = 8
PEER_NKEYS = 128
PEER_DKEY = 256
PEER_TOPK = 16
EPS = 1e-6
LOG2E = 1.4426950408889634

QK_W = RET_HEADS * RET_DK
V_W = RET_HEADS * RET_DV

V7X_VMEM_BYTES = 64 * 1024 * 1024
VMEM_LIMIT_BYTES = V7X_VMEM_BYTES - 8 * 1024 * 1024
LANES = 128
SUBLANES = 8

MIX_CHUNK = 512
HIST = 32
PEER_TOKENS = 512
PEER_EXPERTS = 2048
GATE_ROWS = 32
GATE_GROUP = 2
SCORE_LOOKAHEAD = 2

BF16 = jnp.bfloat16
F32 = jnp.float32


def _dot(a, b):
    return jnp.dot(a, b, preferred_element_type=F32)


def _dot_nt(a, b):
    return jax.lax.dot_general(a, b, (((1,), (1,)), ((), ())), preferred_element_type=F32)


def _sigmoid(x):
    return 1.0 / (1.0 + jnp.exp(-x))


def _mix_chunk(x, cos, sin, refs, state_ref, hbuf_ref):
    (g_mix, w_in, dw_w, dw_b, ln_g, ln_b, conv_proj, ret_proj, w_out, dmask, xi, zeta) = refs
    C = x.shape[0]

    ms = jnp.mean(x * x, axis=-1, keepdims=True)
    xn = (x * jax.lax.rsqrt(ms + EPS) * g_mix[...]).astype(BF16)

    q0 = 2 * CONV_DIM
    k0 = q0 + QK_W
    v0 = k0 + QK_W
    g0 = v0 + V_W
    gc0 = g0 + V_W
    gr0 = gc0 + x.shape[1]
    ab = _dot(xn, w_in[:, 0:q0])
    qk = _dot(xn, w_in[:, q0:v0])
    v_all = _dot(xn, w_in[:, v0:g0]).astype(BF16)
    g_all = _dot(xn, w_in[:, g0:gc0])
    gates = _dot(xn, w_in[:, gc0:gr0 + x.shape[1]])

    hbuf_ref[HIST:HIST + C, :] = ab[:, 0:CONV_DIM] * _sigmoid(ab[:, CONV_DIM:q0])
    y = jnp.zeros((C, CONV_DIM), F32) + dw_b[...]
    for w in range(CONV_WIDTH):
        off = HIST - (CONV_WIDTH - 1) + w
        y = y + hbuf_ref[off:off + C, :] * dw_w[w:w + 1, :]
    hbuf_ref[0:HIST, :] = hbuf_ref[C:C + HIST, :]
    mu = jnp.mean(y, axis=-1, keepdims=True)
    yc = y - mu
    var = jnp.mean(yc * yc, axis=-1, keepdims=True)
    yn = yc * jax.lax.rsqrt(var + EPS) * ln_g[...] + ln_b[...]
    conv_act = (yn * _sigmoid(yn)).astype(BF16)

    heads = []
    for h in range(RET_HEADS):
        qh = qk[:, h * RET_DK:(h + 1) * RET_DK]
        kh = qk[:, QK_W + h * RET_DK:QK_W + (h + 1) * RET_DK]
        vh = v_all[:, h * RET_DV:(h + 1) * RET_DV]
        gh = g_all[:, h * RET_DV:(h + 1) * RET_DV]
        qr = (qh * cos + pltpu.roll(qh, RET_DK // 2, 1) * sin).astype(BF16)
        kr = (kh * cos + pltpu.roll(kh, RET_DK // 2, 1) * sin) * (RET_DK ** -0.5)
        scores = _dot_nt(qr, kr.astype(BF16)) * dmask[h]
        inner = _dot(scores.astype(BF16), vh)
        st = state_ref[h]
        cross = _dot(qr, st.astype(BF16)) * xi[h]
        kz = (kr * zeta[h]).T.astype(BF16)
        state_ref[h] = st * _GAMMA_CHUNK[h] + _dot(kz, vh)
        o = inner + cross
        o = o * jax.lax.rsqrt(jnp.mean(o * o, axis=-1, keepdims=True) + EPS)
        heads.append((o * (gh * _sigmoid(gh))).astype(BF16))
    ret_out = _dot(jnp.concatenate(heads, axis=-1), ret_proj[...])
    conv_out = _dot(conv_act, conv_proj[...])

    gate_c = gates[:, 0:x.shape[1]]
    gate_r = gates[:, x.shape[1]:]
    merged = _sigmoid(gate_c) * conv_out + _sigmoid(gate_r) * ret_out
    return x + _dot(merged.astype(BF16), w_out[...])


def _log_gamma():
    return np.log(1.0 - 2.0 ** (-5.0 - np.arange(RET_HEADS, dtype=np.float64)))


_GAMMA_CHUNK = [float(v) for v in np.exp(_log_gamma() * MIX_CHUNK)]


def _mixer_kernel(x_ref, meta_ref, cos_ref, sin_ref, cosm_ref, sinm_ref, *rest):
    refs = rest[:12]
    out_ref, state_ref, hbuf_ref, state0_ref, hist0_ref = rest[12:]
    b = pl.program_id(0)
    c = pl.program_id(1)

    @pl.when((b == 0) & (c == 0))
    def _():
        state_ref[...] = jnp.zeros_like(state_ref)
        hbuf_ref[0:HIST, :] = jnp.zeros((HIST, CONV_DIM), F32)
        _mix_chunk(meta_ref[...], cosm_ref[...], sinm_ref[...], refs, state_ref, hbuf_ref)
        state0_ref[...] = state_ref[...]
        hist0_ref[...] = hbuf_ref[0:HIST, :]

    @pl.when(c == 0)
    def _():
        state_ref[...] = state0_ref[...]
        hbuf_ref[0:HIST, :] = hist0_ref[...]

    out_ref[0] = _mix_chunk(x_ref[0], cos_ref[...], sin_ref[...], refs, state_ref, hbuf_ref)


def _const_spec(shape):
    nd = len(shape)
    return pl.BlockSpec(shape, lambda b, c, _nd=nd: (0,) * _nd, pipeline_mode=pl.Buffered(1))


def _mixer_call(x, meta_chunk, cos, sin, cos_m, sin_m, consts):
    B, S, D = x.shape
    C = MIX_CHUNK
    assert S % C == 0 and D % LANES == 0
    in_specs = [
        pl.BlockSpec((1, C, D), lambda b, c: (b, c, 0)),
        _const_spec(meta_chunk.shape),
        pl.BlockSpec((C, RET_DK), lambda b, c: (c, 0)),
        pl.BlockSpec((C, RET_DK), lambda b, c: (c, 0)),
        _const_spec(cos_m.shape),
        _const_spec(sin_m.shape),
    ] + [_const_spec(a.shape) for a in consts]
    return pl.pallas_call(
        _mixer_kernel,
        grid=(B, S // C),
        in_specs=in_specs,
        out_specs=pl.BlockSpec((1, C, D), lambda b, c: (b, c, 0)),
        out_shape=jax.ShapeDtypeStruct((B, S, D), F32),
        scratch_shapes=[
            pltpu.VMEM((RET_HEADS, RET_DK, RET_DV), F32),
            pltpu.VMEM((HIST + C, CONV_DIM), F32),
            pltpu.VMEM((RET_HEADS, RET_DK, RET_DV), F32),
            pltpu.VMEM((HIST, CONV_DIM), F32),
        ],
        compiler_params=pltpu.CompilerParams(
            dimension_semantics=("arbitrary", "arbitrary"), vmem_limit_bytes=VMEM_LIMIT_BYTES),
        name="token_mixer",
    )(x, meta_chunk, cos, sin, cos_m, sin_m, *consts)


def _batcher_pairs(n):
    pairs = []
    p = 1
    while p < n:
        k = p
        while k >= 1:
            for j in range(k % p, n - k, 2 * k):
                for i in range(min(k, n - j - k)):
                    if (i + j) // (2 * p) == (i + j + k) // (2 * p):
                        pairs.append((i + j, i + j + k))
            k //= 2
        p *= 2
    return pairs


def _sort_desc(v, pairs):
    v = list(v)
    for i, j in pairs:
        v[i], v[j] = jnp.maximum(v[i], v[j]), jnp.minimum(v[i], v[j])
    return v


def _top_sorted(s, k):
    n, T = s.shape
    assert n == k * SUBLANES and k & (k - 1) == 0
    v = _sort_desc([s[i * SUBLANES:(i + 1) * SUBLANES, :] for i in range(k)], _batcher_pairs(k))
    bitonic = [(i, i + d) for d in (k >> b for b in range(1, k.bit_length())) for i in range(k) if not i & d]
    shift = SUBLANES // 2
    while shift:
        v = _sort_desc([jnp.maximum(v[i], pltpu.roll(v[k - 1 - i], shift, 0)) for i in range(k)], bitonic)
        shift //= 2
    rows = [x[0:1, :] for x in v]
    rows.append(jnp.max(jnp.where(s < rows[-1], s, -jnp.inf), axis=0, keepdims=True))
    return rows


def _gelu_tanh(x):
    inner = x * (0.7978845608028654 + 0.035677408136300125 * (x * x))
    hx = 0.5 * x
    return hx + hx * jnp.tanh(inner)


def _peer_kernel(h_ref, gffn_ref, gfin_ref, wqt_ref, keys_ref, u_ref, vt_ref, out_ref,
                 xt_ref, sb_ref, th_ref, ea_ref, acc_ref, atop_ref, btop_ref):
    s = pl.program_id(1)
    T = h_ref.shape[0]
    K = PEER_TOPK
    NK = PEER_NKEYS
    rows_per_step = u_ref.shape[0] // NK

    @pl.when(s == 0)
    def _():
        x = h_ref[...]
        ms = jnp.mean(x * x, axis=-1, keepdims=True)
        xn = x * jax.lax.rsqrt(ms + EPS) * gffn_ref[...]
        xt_ref[...] = xn.T.astype(BF16)
        qt = _dot(wqt_ref[...], xt_ref[...]).astype(BF16)
        half = PEER_DKEY // 2
        for hp in range(2 * PEER_HEADS):
            h, p = divmod(hp, 2)
            sc = _dot(keys_ref[hp], qt[hp * half:(hp + 1) * half, :])
            top = _top_sorted(sc, K)
            dst = atop_ref if p == 0 else btop_ref
            for r in range(K + 1):
                dst[r, h:h + 1, :] = top[r]
            if p == 0:
                ea = jnp.exp(sc - top[0])
                for lt in range(T // LANES):
                    th_ref[h, lt] = sc[:, lt * LANES:(lt + 1) * LANES]
                    ea_ref[h, lt] = ea[:, lt * LANES:(lt + 1) * LANES]
            else:
                sb_ref[h] = (sc - top[0]) * LOG2E
        a = [atop_ref[r] for r in range(K + 1)]
        bt = [btop_ref[r] for r in range(K + 1)]
        cands = [a[r] + bt[c] for r in range(K + 1) for c in range(K + 1) if (r + 1) * (c + 1) <= K + 1]
        cmax = a[0] + bt[0]
        work = list(cands)
        kth = cmax
        for _ in range(K):
            kth = functools.reduce(jnp.maximum, work)
            work = [jnp.where(v == kth, -jnp.inf, v) for v in work]
        nxt = functools.reduce(jnp.maximum, work)
        tau = 0.5 * (kth + nxt)
        z = functools.reduce(
            lambda p_, q_: p_ + q_, [jnp.where(v >= tau, jnp.exp(v - cmax), 0.0) for v in cands])
        rz = 1.0 / z
        for h in range(PEER_HEADS):
            for lt in range(T // LANES):
                lanes = slice(lt * LANES, (lt + 1) * LANES)
                th_ref[h, lt] = ((tau - bt[0])[h:h + 1, lanes] - th_ref[h, lt]) * LOG2E
                ea_ref[h, lt] = ea_ref[h, lt] * rz[h:h + 1, lanes]
        acc_ref[...] = jnp.zeros_like(acc_ref)

    chunk = GATE_GROUP * NK
    n_chunks = rows_per_step // GATE_GROUP
    scores = lambda g: _dot(u_ref[g * chunk:(g + 1) * chunk, :], xt_ref[...])
    contract = lambda g, wa: _dot(vt_ref[:, g * chunk:(g + 1) * chunk], wa)
    total = None
    acts = {g: scores(g) for g in range(min(SCORE_LOOKAHEAD, n_chunks))}
    wa = None
    for g in range(n_chunks):
        if g + SCORE_LOOKAHEAD < n_chunks:
            acts[g + SCORE_LOOKAHEAD] = scores(g + SCORE_LOOKAHEAD)
        if wa is not None:
            part = contract(g - 1, wa)
            total = part if total is None else total + part
        act = acts.pop(g)
        cols = []
        for lt in range(T // LANES):
            lanes = slice(lt * LANES, (lt + 1) * LANES)
            tiles = [[None] * (NK // GATE_ROWS) for _ in range(GATE_GROUP)]
            i0 = s * rows_per_step + g * GATE_GROUP
            thr = [[th_ref[h, lt, pl.ds(i0 + r, 1), :] for h in range(PEER_HEADS)] for r in range(GATE_GROUP)]
            eav = [[ea_ref[h, lt, pl.ds(i0 + r, 1), :] for h in range(PEER_HEADS)] for r in range(GATE_GROUP)]
            for rt in range(NK // GATE_ROWS):
                rows = slice(rt * GATE_ROWS, (rt + 1) * GATE_ROWS)
                accs = [jnp.zeros((GATE_ROWS, LANES), F32)] * GATE_GROUP
                for h in range(PEER_HEADS):
                    sbt = sb_ref[h, rows, lanes]
                    ebt = jnp.exp2(sbt)
                    for r in range(GATE_GROUP):
                        accs[r] = jnp.where(sbt >= thr[r][h], accs[r] + ebt * eav[r][h], accs[r])
                for r in range(GATE_GROUP):
                    r0 = r * NK + rt * GATE_ROWS
                    tiles[r][rt] = (accs[r] * _gelu_tanh(act[r0:r0 + GATE_ROWS, lanes])).astype(BF16)
            cols.append(jnp.concatenate([t for per_key in tiles for t in per_key], axis=0))
        wa = jnp.concatenate(cols, axis=1)
    part = contract(n_chunks - 1, wa)
    acc_ref[...] += part if total is None else total + part

    @pl.when(s == pl.num_programs(1) - 1)
    def _():
        y = h_ref[...] + acc_ref[...].T
        ms = jnp.mean(y * y, axis=-1, keepdims=True)
        out_ref[...] = y * jax.lax.rsqrt(ms + EPS) * gfin_ref[...]


def _peer_call(h, g_ffn, g_fin, wqt, keys, u, vt):
    N, D = h.shape
    T = PEER_TOKENS
    E = PEER_EXPERTS
    n_exp = u.shape[0]
    assert N % T == 0 and n_exp % E == 0 and E % PEER_NKEYS == 0
    const2 = lambda t, s: (0, 0)
    return pl.pallas_call(
        _peer_kernel,
        grid=(N // T, n_exp // E),
        in_specs=[
            pl.BlockSpec((T, D), lambda t, s: (t, 0)),
            pl.BlockSpec(g_ffn.shape, const2),
            pl.BlockSpec(g_fin.shape, const2),
            pl.BlockSpec(wqt.shape, const2, pipeline_mode=pl.Buffered(1)),
            pl.BlockSpec(keys.shape, lambda t, s: (0, 0, 0), pipeline_mode=pl.Buffered(1)),
            pl.BlockSpec((E, D), lambda t, s: (s, 0)),
            pl.BlockSpec((D, E), lambda t, s: (0, s)),
        ],
        out_specs=pl.BlockSpec((T, D), lambda t, s: (t, 0)),
        out_shape=jax.ShapeDtypeStruct((N, D), F32),
        scratch_shapes=[
            pltpu.VMEM((D, T), BF16),
            pltpu.VMEM((PEER_HEADS, PEER_NKEYS, T), F32),
            pltpu.VMEM((PEER_HEADS, T // LANES, PEER_NKEYS, LANES), F32),
            pltpu.VMEM((PEER_HEADS, T // LANES, PEER_NKEYS, LANES), F32),
            pltpu.VMEM((D, T), F32),
            pltpu.VMEM((PEER_TOPK + 1, PEER_HEADS, T), F32),
            pltpu.VMEM((PEER_TOPK + 1, PEER_HEADS, T), F32),
        ],
        compiler_params=pltpu.CompilerParams(
            dimension_semantics=("arbitrary", "arbitrary"), vmem_limit_bytes=VMEM_LIMIT_BYTES),
        name="peer_dense",
    )(h, g_ffn, g_fin, wqt, keys, u, vt)


def _rotary_tables(pos):
    half = RET_DK // 2
    freqs = ROPE_BASE ** (-jnp.arange(half, dtype=F32) / half)
    ang = pos[:, None] * freqs[None, :]
    cos, sin = jnp.cos(ang), jnp.sin(ang)
    return jnp.concatenate([cos, cos], axis=-1), jnp.concatenate([-sin, sin], axis=-1)


def _decay_tables(C):
    log_gamma = jnp.log(1.0 - 2.0 ** (-5.0 - jnp.arange(RET_HEADS, dtype=F32)))
    i = jnp.arange(C, dtype=F32)
    rel = i[:, None] - i[None, :]
    dmask = jnp.where(rel >= 0, jnp.exp(log_gamma[:, None, None] * jnp.maximum(rel, 0.0)), 0.0)
    xi = jnp.exp(log_gamma[:, None] * (i[None, :] + 1.0))
    zeta = jnp.exp(log_gamma[:, None] * (C - 1.0 - i[None, :]))
    xi = jnp.broadcast_to(xi[:, :, None], (RET_HEADS, C, RET_DV))
    zeta = jnp.broadcast_to(zeta[:, :, None], (RET_HEADS, C, RET_DK))
    return dmask, xi, zeta


def kernel(x, meta_tokens, norm_mix_g, w_in, conv_dw_w, conv_dw_b, conv_ln_g, conv_ln_b, conv_proj_w, ret_proj_w, w_out, norm_ffn_g, peer_w_q, peer_sub_keys, peer_u, peer_v, norm_final_g):
    B, S, D = x.shape
    assert norm_mix_g.shape[0] == 1, "single-layer block"
    C = MIX_CHUNK

    cos, sin = _rotary_tables(jnp.arange(N_META, N_META + S, dtype=F32))
    cos_m, sin_m = _rotary_tables(jnp.arange(N_META - C, N_META, dtype=F32))
    dmask, xi, zeta = _decay_tables(C)
    meta_chunk = jnp.concatenate([jnp.zeros((C - N_META, D), x.dtype), meta_tokens.astype(x.dtype)], axis=0)

    row = lambda v: v.reshape(1, -1).astype(F32)
    consts = (
        row(norm_mix_g[0]), w_in[0].astype(BF16), conv_dw_w[0].astype(F32), row(conv_dw_b[0]),
        row(conv_ln_g[0]), row(conv_ln_b[0]), conv_proj_w[0].astype(BF16), ret_proj_w[0].astype(BF16),
        w_out[0].astype(BF16), dmask, xi, zeta,
    )
    h1 = _mixer_call(x, meta_chunk, cos, sin, cos_m, sin_m, consts)

    keys = peer_sub_keys[0].reshape(2 * PEER_HEADS, PEER_NKEYS, PEER_DKEY // 2).astype(BF16)
    y = _peer_call(
        h1.reshape(B * S, D), row(norm_ffn_g[0]), row(norm_final_g),
        peer_w_q[0].T.astype(BF16), keys, peer_u[0].astype(BF16), peer_v[0].T.astype(BF16))
    return y.reshape(B, S, D)
```

```python
import functools

import numpy as np
import jax
import jax.numpy as jnp
from jax.experimental import pallas as pl
from jax.experimental.pallas import tpu as pltpu

N_META = 16
CONV_DIM = 512
CONV_WIDTH = 31
RET_HEADS = 4
RET_DK = 128
RET_DV = 256
ROPE_BASE = 10000.0
PEER_HEADS = 8
PEER_NKEYS = 128
PEER_DKEY = 256
PEER_TOPK = 16
EPS = 1e-6
LOG2E = 1.4426950408889634

QK_W = RET_HEADS * RET_DK
V_W = RET_HEADS * RET_DV

V7X_VMEM_BYTES = 64 * 1024 * 1024
VMEM_LIMIT_BYTES = V7X_VMEM_BYTES - 8 * 1024 * 1024
LANES = 128
SUBLANES = 8

MIX_CHUNK = 512
HIST = 32
PEER_TOKENS = 512
PEER_EXPERTS = 2048
GATE_ROWS = 32
GATE_GROUP = 4
TOKEN_PARTS = 2
SCORE_LOOKAHEAD = 2

BF16 = jnp.bfloat16
F32 = jnp.float32


def _dot(a, b):
    return jnp.dot(a, b, preferred_element_type=F32)


def _dot_nt(a, b):
    return jax.lax.dot_general(a, b, (((1,), (1,)), ((), ())), preferred_element_type=F32)


def _sigmoid(x):
    return 1.0 / (1.0 + jnp.exp(-x))


def _mix_chunk(x, cos, sin, refs, state_ref, hbuf_ref):
    (g_mix, w_in, dw_w, dw_b, ln_g, ln_b, conv_proj, ret_proj, w_out, dmask, xi, zeta) = refs
    C = x.shape[0]

    ms = jnp.mean(x * x, axis=-1, keepdims=True)
    xn = (x * jax.lax.rsqrt(ms + EPS) * g_mix[...]).astype(BF16)

    q0 = 2 * CONV_DIM
    k0 = q0 + QK_W
    v0 = k0 + QK_W
    g0 = v0 + V_W
    gc0 = g0 + V_W
    gr0 = gc0 + x.shape[1]
    ab = _dot(xn, w_in[:, 0:q0])
    qk = _dot(xn, w_in[:, q0:v0])
    v_all = _dot(xn, w_in[:, v0:g0]).astype(BF16)
    g_all = _dot(xn, w_in[:, g0:gc0])
    gates = _dot(xn, w_in[:, gc0:gr0 + x.shape[1]])

    hbuf_ref[HIST:HIST + C, :] = ab[:, 0:CONV_DIM] * _sigmoid(ab[:, CONV_DIM:q0])
    y = jnp.zeros((C, CONV_DIM), F32) + dw_b[...]
    for w in range(CONV_WIDTH):
        off = HIST - (CONV_WIDTH - 1) + w
        y = y + hbuf_ref[off:off + C, :] * dw_w[w:w + 1, :]
    hbuf_ref[0:HIST, :] = hbuf_ref[C:C + HIST, :]
    mu = jnp.mean(y, axis=-1, keepdims=True)
    yc = y - mu
    var = jnp.mean(yc * yc, axis=-1, keepdims=True)
    yn = yc * jax.lax.rsqrt(var + EPS) * ln_g[...] + ln_b[...]
    conv_act = (yn * _sigmoid(yn)).astype(BF16)

    heads = []
    for h in range(RET_HEADS):
        qh = qk[:, h * RET_DK:(h + 1) * RET_DK]
        kh = qk[:, QK_W + h * RET_DK:QK_W + (h + 1) * RET_DK]
        vh = v_all[:, h * RET_DV:(h + 1) * RET_DV]
        gh = g_all[:, h * RET_DV:(h + 1) * RET_DV]
        qr = (qh * cos + pltpu.roll(qh, RET_DK // 2, 1) * sin).astype(BF16)
        kr = (kh * cos + pltpu.roll(kh, RET_DK // 2, 1) * sin) * (RET_DK ** -0.5)
        scores = _dot_nt(qr, kr.astype(BF16)) * dmask[h]
        inner = _dot(scores.astype(BF16), vh)
        st = state_ref[h]
        cross = _dot(qr, st.astype(BF16)) * xi[h]
        kz = (kr * zeta[h]).T.astype(BF16)
        state_ref[h] = st * _GAMMA_CHUNK[h] + _dot(kz, vh)
        o = inner + cross
        o = o * jax.lax.rsqrt(jnp.mean(o * o, axis=-1, keepdims=True) + EPS)
        heads.append((o * (gh * _sigmoid(gh))).astype(BF16))
    ret_out = _dot(jnp.concatenate(heads, axis=-1), ret_proj[...])
    conv_out = _dot(conv_act, conv_proj[...])

    gate_c = gates[:, 0:x.shape[1]]
    gate_r = gates[:, x.shape[1]:]
    merged = _sigmoid(gate_c) * conv_out + _sigmoid(gate_r) * ret_out
    return x + _dot(merged.astype(BF16), w_out[...])


def _log_gamma():
    return np.log(1.0 - 2.0 ** (-5.0 - np.arange(RET_HEADS, dtype=np.float64)))


_GAMMA_CHUNK = [float(v) for v in np.exp(_log_gamma() * MIX_CHUNK)]


def _mixer_kernel(x_ref, meta_ref, cos_ref, sin_ref, cosm_ref, sinm_ref, *rest):
    refs = rest[:12]
    out_ref, state_ref, hbuf_ref, state0_ref, hist0_ref = rest[12:]
    b = pl.program_id(0)
    c = pl.program_id(1)

    @pl.when((b == 0) & (c == 0))
    def _():
        state_ref[...] = jnp.zeros_like(state_ref)
        hbuf_ref[0:HIST, :] = jnp.zeros((HIST, CONV_DIM), F32)
        _mix_chunk(meta_ref[...], cosm_ref[...], sinm_ref[...], refs, state_ref, hbuf_ref)
        state0_ref[...] = state_ref[...]
        hist0_ref[...] = hbuf_ref[0:HIST, :]

    @pl.when(c == 0)
    def _():
        state_ref[...] = state0_ref[...]
        hbuf_ref[0:HIST, :] = hist0_ref[...]

    out_ref[0] = _mix_chunk(x_ref[0], cos_ref[...], sin_ref[...], refs, state_ref, hbuf_ref)


def _const_spec(shape):
    nd = len(shape)
    return pl.BlockSpec(shape, lambda b, c, _nd=nd: (0,) * _nd, pipeline_mode=pl.Buffered(1))


def _mixer_call(x, meta_chunk, cos, sin, cos_m, sin_m, consts):
    B, S, D = x.shape
    C = MIX_CHUNK
    assert S % C == 0 and D % LANES == 0
    in_specs = [
        pl.BlockSpec((1, C, D), lambda b, c: (b, c, 0)),
        _const_spec(meta_chunk.shape),
        pl.BlockSpec((C, RET_DK), lambda b, c: (c, 0)),
        pl.BlockSpec((C, RET_DK), lambda b, c: (c, 0)),
        _const_spec(cos_m.shape),
        _const_spec(sin_m.shape),
    ] + [_const_spec(a.shape) for a in consts]
    return pl.pallas_call(
        _mixer_kernel,
        grid=(B, S // C),
        in_specs=in_specs,
        out_specs=pl.BlockSpec((1, C, D), lambda b, c: (b, c, 0)),
        out_shape=jax.ShapeDtypeStruct((B, S, D), F32),
        scratch_shapes=[
            pltpu.VMEM((RET_HEADS, RET_DK, RET_DV), F32),
            pltpu.VMEM((HIST + C, CONV_DIM), F32),
            pltpu.VMEM((RET_HEADS, RET_DK, RET_DV), F32),
            pltpu.VMEM((HIST, CONV_DIM), F32),
        ],
        compiler_params=pltpu.CompilerParams(
            dimension_semantics=("arbitrary", "arbitrary"), vmem_limit_bytes=VMEM_LIMIT_BYTES),
        name="token_mixer",
    )(x, meta_chunk, cos, sin, cos_m, sin_m, *consts)


def _batcher_pairs(n):
    pairs = []
    p = 1
    while p < n:
        k = p
        while k >= 1:
            for j in range(k % p, n - k, 2 * k):
                for i in range(min(k, n - j - k)):
                    if (i + j) // (2 * p) == (i + j + k) // (2 * p):
                        pairs.append((i + j, i + j + k))
            k //= 2
        p *= 2
    return pairs


def _sort_desc(v, pairs):
    v = list(v)
    for i, j in pairs:
        v[i], v[j] = jnp.maximum(v[i], v[j]), jnp.minimum(v[i], v[j])
    return v


def _top_sorted(s, k):
    n, T = s.shape
    assert n == k * SUBLANES and k & (k - 1) == 0
    v = _sort_desc([s[i * SUBLANES:(i + 1) * SUBLANES, :] for i in range(k)], _batcher_pairs(k))
    bitonic = [(i, i + d) for d in (k >> b for b in range(1, k.bit_length())) for i in range(k) if not i & d]
    shift = SUBLANES // 2
    while shift:
        v = _sort_desc([jnp.maximum(v[i], pltpu.roll(v[k - 1 - i], shift, 0)) for i in range(k)], bitonic)
        shift //= 2
    rows = [x[0:1, :] for x in v]
    rows.append(jnp.max(jnp.where(s < rows[-1], s, -jnp.inf), axis=0, keepdims=True))
    return rows


def _gelu_tanh(x):
    inner = x * (0.7978845608028654 + 0.035677408136300125 * (x * x))
    hx = 0.5 * x
    return hx + hx * jnp.tanh(inner)


def _peer_kernel(h_ref, gffn_ref, gfin_ref, wqt_ref, keys_ref, u_ref, vt_ref, out_ref,
                 xt_ref, sb_ref, th_ref, ea_ref, acc_ref, atop_ref, btop_ref):
    s = pl.program_id(1)
    T = h_ref.shape[0]
    K = PEER_TOPK
    NK = PEER_NKEYS
    rows_per_step = u_ref.shape[0] // NK

    @pl.when(s == 0)
    def _():
        x = h_ref[...]
        ms = jnp.mean(x * x, axis=-1, keepdims=True)
        xn = x * jax.lax.rsqrt(ms + EPS) * gffn_ref[...]
        xt_ref[...] = xn.T.astype(BF16)
        qt = _dot(wqt_ref[...], xt_ref[...]).astype(BF16)
        half = PEER_DKEY // 2
        for hp in range(2 * PEER_HEADS):
            h, p = divmod(hp, 2)
            sc = _dot(keys_ref[hp], qt[hp * half:(hp + 1) * half, :])
            top = _top_sorted(sc, K)
            dst = atop_ref if p == 0 else btop_ref
            for r in range(K + 1):
                dst[r, h:h + 1, :] = top[r]
            if p == 0:
                ea = jnp.exp(sc - top[0])
                for lt in range(T // LANES):
                    th_ref[h, lt] = sc[:, lt * LANES:(lt + 1) * LANES]
                    ea_ref[h, lt] = ea[:, lt * LANES:(lt + 1) * LANES]
            else:
                sb_ref[h] = (sc - top[0]) * LOG2E
        a = [atop_ref[r] for r in range(K + 1)]
        bt = [btop_ref[r] for r in range(K + 1)]
        cands = [a[r] + bt[c] for r in range(K + 1) for c in range(K + 1) if (r + 1) * (c + 1) <= K + 1]
        cmax = a[0] + bt[0]
        work = list(cands)
        kth = cmax
        for _ in range(K):
            kth = functools.reduce(jnp.maximum, work)
            work = [jnp.where(v == kth, -jnp.inf, v) for v in work]
        nxt = functools.reduce(jnp.maximum, work)
        tau = 0.5 * (kth + nxt)
        z = functools.reduce(
            lambda p_, q_: p_ + q_, [jnp.where(v >= tau, jnp.exp(v - cmax), 0.0) for v in cands])
        rz = 1.0 / z
        for h in range(PEER_HEADS):
            for lt in range(T // LANES):
                lanes = slice(lt * LANES, (lt + 1) * LANES)
                th_ref[h, lt] = ((tau - bt[0])[h:h + 1, lanes] - th_ref[h, lt]) * LOG2E
                ea_ref[h, lt] = ea_ref[h, lt] * rz[h:h + 1, lanes]
        acc_ref[...] = jnp.zeros_like(acc_ref)

    chunk = GATE_GROUP * NK
    n_chunks = rows_per_step // GATE_GROUP
    part_w = T // TOKEN_PARTS
    items = [(g, tp) for g in range(n_chunks) for tp in range(TOKEN_PARTS)]
    scores = lambda g, tp: _dot(u_ref[g * chunk:(g + 1) * chunk, :], xt_ref[:, tp * part_w:(tp + 1) * part_w])
    contract = lambda g, wa: _dot(vt_ref[:, g * chunk:(g + 1) * chunk], wa)
    totals = [None] * TOKEN_PARTS
    acts = {k: scores(*items[k]) for k in range(min(SCORE_LOOKAHEAD, len(items)))}
    wa = None
    for k, (g, tp) in enumerate(items):
        if k + SCORE_LOOKAHEAD < len(items):
            acts[k + SCORE_LOOKAHEAD] = scores(*items[k + SCORE_LOOKAHEAD])
        if wa is not None:
            pg, ptp = items[k - 1]
            part = contract(pg, wa)
            totals[ptp] = part if totals[ptp] is None else totals[ptp] + part
        act = acts.pop(k)
        cols = []
        for lt in range(tp * part_w // LANES, (tp + 1) * part_w // LANES):
            lanes = slice(lt * LANES, (lt + 1) * LANES)
            act_lanes = slice(lt * LANES - tp * part_w, (lt + 1) * LANES - tp * part_w)
            tiles = [[None] * (NK // GATE_ROWS) for _ in range(GATE_GROUP)]
            i0 = s * rows_per_step + g * GATE_GROUP
            thr = [[th_ref[h, lt, pl.ds(i0 + r, 1), :] for h in range(PEER_HEADS)] for r in range(GATE_GROUP)]
            eav = [[ea_ref[h, lt, pl.ds(i0 + r, 1), :] for h in range(PEER_HEADS)] for r in range(GATE_GROUP)]
            for rt in range(NK // GATE_ROWS):
                rows = slice(rt * GATE_ROWS, (rt + 1) * GATE_ROWS)
                accs = [jnp.zeros((GATE_ROWS, LANES), F32)] * GATE_GROUP
                for h in range(PEER_HEADS):
                    sbt = sb_ref[h, rows, lanes]
                    ebt = jnp.exp2(sbt)
                    for r in range(GATE_GROUP):
                        accs[r] = jnp.where(sbt >= thr[r][h], accs[r] + ebt * eav[r][h], accs[r])
                for r in range(GATE_GROUP):
                    r0 = r * NK + rt * GATE_ROWS
                    tiles[r][rt] = (accs[r] * _gelu_tanh(act[r0:r0 + GATE_ROWS, act_lanes])).astype(BF16)
            cols.append(jnp.concatenate([t for per_key in tiles for t in per_key], axis=0))
        wa = jnp.concatenate(cols, axis=1)
    pg, ptp = items[-1]
    part = contract(pg, wa)
    totals[ptp] = part if totals[ptp] is None else totals[ptp] + part
    for tp in range(TOKEN_PARTS):
        acc_ref[:, tp * part_w:(tp + 1) * part_w] += totals[tp]

    @pl.when(s == pl.num_programs(1) - 1)
    def _():
        y = h_ref[...] + acc_ref[...].T
        ms = jnp.mean(y * y, axis=-1, keepdims=True)
        out_ref[...] = y * jax.lax.rsqrt(ms + EPS) * gfin_ref[...]


def _peer_call(h, g_ffn, g_fin, wqt, keys, u, vt):
    N, D = h.shape
    T = PEER_TOKENS
    E = PEER_EXPERTS
    n_exp = u.shape[0]
    assert N % T == 0 and n_exp % E == 0 and E % PEER_NKEYS == 0
    const2 = lambda t, s: (0, 0)
    return pl.pallas_call(
        _peer_kernel,
        grid=(N // T, n_exp // E),
        in_specs=[
            pl.BlockSpec((T, D), lambda t, s: (t, 0)),
            pl.BlockSpec(g_ffn.shape, const2),
            pl.BlockSpec(g_fin.shape, const2),
            pl.BlockSpec(wqt.shape, const2, pipeline_mode=pl.Buffered(1)),
            pl.BlockSpec(keys.shape, lambda t, s: (0, 0, 0), pipeline_mode=pl.Buffered(1)),
            pl.BlockSpec((E, D), lambda t, s: (s, 0)),
            pl.BlockSpec((D, E), lambda t, s: (0, s)),
        ],
        out_specs=pl.BlockSpec((T, D), lambda t, s: (t, 0)),
        out_shape=jax.ShapeDtypeStruct((N, D), F32),
        scratch_shapes=[
            pltpu.VMEM((D, T), BF16),
            pltpu.VMEM((PEER_HEADS, PEER_NKEYS, T), F32),
            pltpu.VMEM((PEER_HEADS, T // LANES, PEER_NKEYS, LANES), F32),
            pltpu.VMEM((PEER_HEADS, T // LANES, PEER_NKEYS, LANES), F32),
            pltpu.VMEM((D, T), F32),
            pltpu.VMEM((PEER_TOPK + 1, PEER_HEADS, T), F32),
            pltpu.VMEM((PEER_TOPK + 1, PEER_HEADS, T), F32),
        ],
        compiler_params=pltpu.CompilerParams(
            dimension_semantics=("arbitrary", "arbitrary"), vmem_limit_bytes=VMEM_LIMIT_BYTES),
        name="peer_dense",
    )(h, g_ffn, g_fin, wqt, keys, u, vt)


def _rotary_tables(pos):
    half = RET_DK // 2
    freqs = ROPE_BASE ** (-jnp.arange(half, dtype=F32) / half)
    ang = pos[:, None] * freqs[None, :]
    cos, sin = jnp.cos(ang), jnp.sin(ang)
    return jnp.concatenate([cos, cos], axis=-1), jnp.concatenate([-sin, sin], axis=-1)


def _decay_tables(C):
    log_gamma = jnp.log(1.0 - 2.0 ** (-5.0 - jnp.arange(RET_HEADS, dtype=F32)))
    i = jnp.arange(C, dtype=F32)
    rel = i[:, None] - i[None, :]
    dmask = jnp.where(rel >= 0, jnp.exp(log_gamma[:, None, None] * jnp.maximum(rel, 0.0)), 0.0)
    xi = jnp.exp(log_gamma[:, None] * (i[None, :] + 1.0))
    zeta = jnp.exp(log_gamma[:, None] * (C - 1.0 - i[None, :]))
    xi = jnp.broadcast_to(xi[:, :, None], (RET_HEADS, C, RET_DV))
    zeta = jnp.broadcast_to(zeta[:, :, None], (RET_HEADS, C, RET_DK))
    return dmask, xi, zeta


def kernel(x, meta_tokens, norm_mix_g, w_in, conv_dw_w, conv_dw_b, conv_ln_g, conv_ln_b, conv_proj_w, ret_proj_w, w_out, norm_ffn_g, peer_w_q, peer_sub_keys, peer_u, peer_v, norm_final_g):
    B, S, D = x.shape
    assert norm_mix_g.shape[0] == 1, "single-layer block"
    C = MIX_CHUNK

    cos, sin = _rotary_tables(jnp.arange(N_META, N_META + S, dtype=F32))
    cos_m, sin_m = _rotary_tables(jnp.arange(N_META - C, N_META, dtype=F32))
    dmask, xi, zeta = _decay_tables(C)
    meta_chunk = jnp.concatenate([jnp.zeros((C - N_META, D), x.dtype), meta_tokens.astype(x.dtype)], axis=0)

    row = lambda v: v.reshape(1, -1).astype(F32)
    consts = (
        row(norm_mix_g[0]), w_in[0].astype(BF16), conv_dw_w[0].astype(F32), row(conv_dw_b[0]),
        row(conv_ln_g[0]), row(conv_ln_b[0]), conv_proj_w[0].astype(BF16), ret_proj_w[0].astype(BF16),
        w_out[0].astype(BF16), dmask, xi, zeta,
    )
    h1 = _mixer_call(x, meta_chunk, cos, sin, cos_m, sin_m, consts)

    keys = peer_sub_keys[0].reshape(2 * PEER_HEADS, PEER_NKEYS, PEER_DKEY // 2).astype(BF16)
    y = _peer_call(
        h1.reshape(B * S, D), row(norm_ffn_g[0]), row(norm_final_g),
        peer_w_q[0].T.astype(BF16), keys, peer_u[0].astype(BF16), peer_v[0].T.astype(BF16))
    return y.reshape(B, S, D)
```
